```python
import jax, jax.numpy as jnp
from jax import lax
import numpy as np

D_MODEL = 2048
BATCH = 2
SEQ = 16384
DEPTH = 1
DEC_BATCH = 32
DEC_SEQ = 32
PAST_LEN = 4096

CHUNK = 64
D_MIX = D_MODEL
RET_HEADS = 8
RET_DIM = 128
RET_WIDTH = RET_HEADS * RET_DIM
SSM_HEADS = 16
SSM_HEADDIM = 64
SSM_WIDTH = SSM_HEADS * SSM_HEADDIM
SSM_GROUPS = 2
HEADS_PER_GROUP = SSM_HEADS // SSM_GROUPS
SSM_STATE = 128
CONV_WIDTH = 4
CONV_DIM = SSM_WIDTH + 2 * SSM_GROUPS * SSM_STATE
D_FF = 4 * D_MODEL
ROPE_BASE = 10000.0
EPS = 1e-6
IN_COLS = 4 * RET_WIDTH + SSM_WIDTH + CONV_DIM + SSM_HEADS
IN_SPLITS = (RET_WIDTH, 2 * RET_WIDTH, 3 * RET_WIDTH, 4 * RET_WIDTH,
             4 * RET_WIDTH + SSM_WIDTH, 4 * RET_WIDTH + SSM_WIDTH + CONV_DIM)

kernel_name = 'hybrid_retention_ssd_stream_step'


def rms_norm(x, w):
    xf = x.astype(jnp.float32)
    y = xf * lax.rsqrt(jnp.mean(xf * xf, axis=-1, keepdims=True) + EPS)
    return (y * w.astype(jnp.float32)).astype(x.dtype)


def rotary(x, pos):
    half = RET_DIM // 2
    inv = ROPE_BASE ** (-jnp.arange(half, dtype=jnp.float32) / half)
    ang = pos.astype(jnp.float32)[:, None] * inv[None, :]
    cos = jnp.cos(ang)[None, :, None, :]
    sin = jnp.sin(ang)[None, :, None, :]
    x1, x2 = x[..., :half], x[..., half:]
    return jnp.concatenate([x1 * cos - x2 * sin, x2 * cos + x1 * sin], axis=-1)


def head_layernorm(o):
    mu = jnp.mean(o, axis=-1, keepdims=True)
    d = o - mu
    return d * lax.rsqrt(jnp.mean(d * d, axis=-1, keepdims=True) + EPS)


def causal_conv(u, buf, w, bias):
    L = u.shape[1]
    padded = jnp.concatenate([buf.astype(u.dtype), u], axis=1)
    out = bias
    for j in range(CONV_WIDTH):
        out = out + padded[:, j:j + L] * w[j]
    return jax.nn.silu(out), padded[:, -(CONV_WIDTH - 1):]


def retention_block(q, k, v, S, log_gamma):
    L = q.shape[1]
    idx = jnp.arange(L, dtype=jnp.float32)
    lg = log_gamma[:, None, None]
    intra = jnp.exp(lg * jnp.abs(idx[:, None] - idx[None, :]))
    scores = jnp.einsum('bihd,bjhd->bhij', q, k) * intra[None]
    o = jnp.einsum('bhij,bjhd->bihd', scores, v)
    q_decay = jnp.exp(log_gamma[None, :] * (idx[:, None] + 1.0))
    o = o + jnp.einsum('bihd,bhde->bihe', q, S) * q_decay[None, :, :, None]
    k_decay = jnp.exp(log_gamma[None, :] * (L - 1.0 - idx[:, None]))
    S_new = jnp.exp(log_gamma * L)[None, :, None, None] * S + jnp.einsum(
        'bjhd,bjhe->bhde', k * k_decay[None, :, :, None], v)
    return o, S_new


def ssd_block(x, dt, A, Bm, Cm, h):
    L = x.shape[1]
    Bh = jnp.repeat(Bm, HEADS_PER_GROUP, axis=2)
    Ch = jnp.repeat(Cm, HEADS_PER_GROUP, axis=2)
    cum = jnp.cumsum(dt * A, axis=1)
    causal = jnp.tril(jnp.ones((L, L), dtype=bool))[None, :, :, None]
    seg = jnp.where(causal, cum[:, :, None, :] - cum[:, None, :, :], -jnp.inf)
    w = jnp.einsum('bihn,bjhn->bijh', Ch, Bh) * jnp.exp(seg) * dt[:, None, :, :]
    y = jnp.einsum('bijh,bjhp->bihp', w, x) + jnp.einsum(
        'bihn,bhpn->bihp', Ch, h) * jnp.exp(cum)[..., None]
    to_end = jnp.exp(cum[:, -1:, :] - cum) * dt
    h_new = jnp.exp(cum[:, -1, :])[:, :, None, None] * h + jnp.einsum(
        'bjh,bjhn,bjhp->bhpn', to_end, Bh, x)
    return y, h_new


def mixer_blocks(q, k, v, xs, dt, Bm, Cm, S0, h0, log_gamma, A):
    b, L = q.shape[0], q.shape[1]
    blk = min(L, CHUNK)
    nb = L // blk

    def to_blocks(a):
        return jnp.swapaxes(a.reshape((b, nb, blk) + a.shape[2:]), 0, 1)

    def from_blocks(a):
        return jnp.swapaxes(a, 0, 1).reshape((b, L) + a.shape[3:])

    def step(carry, inp):
        S, h = carry
        qb, kb, vb, xb, dtb, Bb, Cb = inp
        o, S = retention_block(qb, kb, vb, S, log_gamma)
        y, h = ssd_block(xb, dtb, A, Bb, Cb, h)
        return (S, h), (o, y)

    inputs = (to_blocks(q), to_blocks(k), to_blocks(v), to_blocks(xs),
              to_blocks(dt), to_blocks(Bm), to_blocks(Cm))
    (S, h), (o, y) = lax.scan(step, (S0, h0), inputs)
    return from_blocks(o), from_blocks(y), S, h


def trunk_layer(x, c, pos, S0, h0, conv_buf, w_ada, b_ada, pre_mix_w, post_mix_w,
                pre_ffn_w, post_ffn_w, w_in, conv_w, conv_b, dt_bias, a_log, d_skip,
                ssm_norm_w, w_out, w_up, w_down):
    b, L, _ = x.shape
    f32 = jnp.float32
    mod = jnp.einsum('bd,de->be', jax.nn.silu(c), w_ada) + b_ada
    sh1, sc1, g1, sh2, sc2, g2 = jnp.split(mod[:, None, :].astype(x.dtype), 6, axis=-1)

    h = rms_norm(x, pre_mix_w) * (1 + sc1) + sh1
    proj = jnp.einsum('bld,de->ble', h, w_in)
    q, k, v, g, z, xbc, dt_raw = jnp.split(proj, IN_SPLITS, axis=-1)

    q = rotary(q.reshape(b, L, RET_HEADS, RET_DIM).astype(f32), pos)
    k = rotary(k.reshape(b, L, RET_HEADS, RET_DIM).astype(f32), pos) * (RET_DIM ** -0.5)
    v = v.reshape(b, L, RET_HEADS, RET_DIM).astype(f32)
    log_gamma = jnp.log(1.0 - 2.0 ** (-5.0 - jnp.arange(RET_HEADS, dtype=f32)))

    xbc_act, conv_new = causal_conv(xbc, conv_buf, conv_w, conv_b)
    xs, Bm, Cm = jnp.split(xbc_act.astype(f32), (SSM_WIDTH, SSM_WIDTH + SSM_GROUPS * SSM_STATE), axis=-1)
    xs = xs.reshape(b, L, SSM_HEADS, SSM_HEADDIM)
    Bm = Bm.reshape(b, L, SSM_GROUPS, SSM_STATE)
    Cm = Cm.reshape(b, L, SSM_GROUPS, SSM_STATE)
    dt = jax.nn.softplus(dt_raw.astype(f32) + dt_bias.astype(f32))
    A = -jnp.exp(a_log.astype(f32))

    o_ret, y_ssm, S_new, h_new = mixer_blocks(q, k, v, xs, dt, Bm, Cm,
                                              S0.astype(f32), h0.astype(f32), log_gamma, A)

    ret_out = head_layernorm(o_ret).reshape(b, L, RET_WIDTH) * jax.nn.silu(g.astype(f32))
    y_ssm = y_ssm + d_skip.astype(f32)[:, None] * xs
    yz = (y_ssm.reshape(b, L, SSM_WIDTH) * jax.nn.silu(z.astype(f32))).reshape(b, L, SSM_GROUPS, -1)
    yz = yz * lax.rsqrt(jnp.mean(yz * yz, axis=-1, keepdims=True) + EPS)
    ssm_out = yz.reshape(b, L, SSM_WIDTH) * ssm_norm_w.astype(f32)

    mix = jnp.concatenate([ret_out, ssm_out], axis=-1).astype(x.dtype)
    mix = jnp.einsum('ble,ed->bld', mix, w_out)
    x = x + g1 * rms_norm(mix, post_mix_w)

    h2 = rms_norm(x, pre_ffn_w) * (1 + sc2) + sh2
    u = jnp.square(jax.nn.relu(jnp.einsum('bld,df->blf', h2, w_up)))
    f = jnp.einsum('blf,fd->bld', u, w_down)
    x = x + g2 * rms_norm(f, post_ffn_w)
    return x, S_new, h_new, conv_new


def setup_inputs(seed: int = 0) -> dict:
    key = jax.random.key(seed)
    ks = jax.random.split(key, 32)
    nrm = lambda k, s, sc: jax.random.normal(k, s, jnp.float32) * sc
    dt0 = jnp.exp(jax.random.uniform(ks[20], (DEPTH, SSM_HEADS), jnp.float32,
                                     np.log(1e-3), np.log(1e-1)))
    return {
        'x_prompt': nrm(ks[0], (BATCH, SEQ, D_MODEL), 1.0),
        'x_sample': nrm(ks[1], (DEC_BATCH, DEC_SEQ, D_MODEL), 1.0),
        'state_ret': nrm(ks[2], (DEPTH, DEC_BATCH, RET_HEADS, RET_DIM, RET_DIM), 0.1),
        'state_ssm': nrm(ks[3], (DEPTH, DEC_BATCH, SSM_HEADS, SSM_HEADDIM, SSM_STATE), 0.1),
        'state_conv': nrm(ks[4], (DEPTH, DEC_BATCH, CONV_WIDTH - 1, CONV_DIM), 1.0),
        'c_prompt': nrm(ks[5], (BATCH, D_MODEL), 1.0),
        'c_sample': nrm(ks[6], (DEC_BATCH, D_MODEL), 1.0),
        'w_ada': nrm(ks[7], (DEPTH, D_MODEL, 6 * D_MODEL), 0.5 * D_MODEL ** -0.5),
        'b_ada': nrm(ks[8], (DEPTH, 6 * D_MODEL), 0.01),
        'pre_mix_w': 1.0 + nrm(ks[9], (DEPTH, D_MODEL), 0.01),
        'post_mix_w': 1.0 + nrm(ks[10], (DEPTH, D_MODEL), 0.01),
        'pre_ffn_w': 1.0 + nrm(ks[11], (DEPTH, D_MODEL), 0.01),
        'post_ffn_w': 1.0 + nrm(ks[12], (DEPTH, D_MODEL), 0.01),
        'w_in': nrm(ks[13], (DEPTH, D_MODEL, IN_COLS), D_MODEL ** -0.5),
        'conv_w': nrm(ks[14], (DEPTH, CONV_WIDTH, CONV_DIM), CONV_WIDTH ** -0.5),
        'conv_b': nrm(ks[15], (DEPTH, CONV_DIM), 0.01),
        'dt_bias': dt0 + jnp.log(-jnp.expm1(-dt0)),
        'a_log': jnp.log(jax.random.uniform(ks[16], (DEPTH, SSM_HEADS), jnp.float32, 1.0, 16.0)),
        'd_skip': 1.0 + nrm(ks[17], (DEPTH, SSM_HEADS), 0.01),
        'ssm_norm_w': 1.0 + nrm(ks[18], (DEPTH, SSM_WIDTH), 0.01),
        'w_out': nrm(ks[19], (DEPTH, D_MIX, D_MODEL), D_MIX ** -0.5),
        'w_up': nrm(ks[21], (DEPTH, D_MODEL, D_FF), D_MODEL ** -0.5),
        'w_down': nrm(ks[22], (DEPTH, D_FF, D_MODEL), D_FF ** -0.5),
    }


def reference(x_prompt, x_sample, state_ret, state_ssm, state_conv, c_prompt, c_sample,
              w_ada, b_ada, pre_mix_w, post_mix_w, pre_ffn_w, post_ffn_w, w_in, conv_w,
              conv_b, dt_bias, a_log, d_skip, ssm_norm_w, w_out, w_up, w_down):
    bp = x_prompt.shape[0]
    pos_p = jnp.arange(x_prompt.shape[1])
    pos_s = PAST_LEN + jnp.arange(x_sample.shape[1])
    S0_p = jnp.zeros((bp, RET_HEADS, RET_DIM, RET_DIM), jnp.float32)
    h0_p = jnp.zeros((bp, SSM_HEADS, SSM_HEADDIM, SSM_STATE), jnp.float32)
    conv0_p = jnp.zeros((bp, CONV_WIDTH - 1, CONV_DIM), x_prompt.dtype)
    yp, ys = x_prompt, x_sample
    rp, sp, cp, rs, ss, cs = [], [], [], [], [], []
    for l in range(DEPTH):
        lw = (w_ada[l], b_ada[l], pre_mix_w[l], post_mix_w[l], pre_ffn_w[l], post_ffn_w[l],
              w_in[l], conv_w[l], conv_b[l], dt_bias[l], a_log[l], d_skip[l], ssm_norm_w[l],
              w_out[l], w_up[l], w_down[l])
        yp, r_new, s_new, c_new = trunk_layer(yp, c_prompt, pos_p, S0_p, h0_p, conv0_p, *lw)
        rp.append(r_new); sp.append(s_new); cp.append(c_new)
        ys, r_new, s_new, c_new = trunk_layer(ys, c_sample, pos_s, state_ret[l], state_ssm[l],
                                              state_conv[l], *lw)
        rs.append(r_new); ss.append(s_new); cs.append(c_new)
    return (yp, ys, jnp.stack(rp), jnp.stack(sp), jnp.stack(cp),
            jnp.stack(rs), jnp.stack(ss), jnp.stack(cs))
```

```python
import functools
import math

import jax
import jax.numpy as jnp
from jax import lax
from jax.experimental import pallas as pl
from jax.experimental.pallas import tpu as pltpu

F32 = jnp.float32
BF16 = jnp.bfloat16

D_MODEL = 2048
PAST_LEN = 4096
CHUNK = 64
RET_HEADS = 8
RET_DIM = 128
RET_WIDTH = RET_HEADS * RET_DIM
SSM_HEADS = 16
SSM_HEADDIM = 64
SSM_WIDTH = SSM_HEADS * SSM_HEADDIM
SSM_GROUPS = 2
HEADS_PER_GROUP = SSM_HEADS // SSM_GROUPS
SSM_STATE = 128
CONV_WIDTH = 4
CONV_DIM = SSM_WIDTH + 2 * SSM_GROUPS * SSM_STATE
D_FF = 4 * D_MODEL
ROPE_BASE = 10000.0
EPS = 1e-6

LANES = 128
CONV_PAD = 8
OFF_Q = 0
OFF_K = RET_WIDTH
OFF_V = 2 * RET_WIDTH
OFF_G = 3 * RET_WIDTH
OFF_Z = 4 * RET_WIDTH
OFF_XBC = 4 * RET_WIDTH + SSM_WIDTH
OFF_DT = OFF_XBC + CONV_DIM
IN_COLS = OFF_DT + SSM_HEADS
IN_COLS_PAD = OFF_DT + LANES
PROJ_BLOCK = 512

VMEM_LIMIT = 60 * 1024 * 1024


def _silu(x):
    return x * (1.0 / (1.0 + jnp.exp(-x)))


def _softplus(x):
    return jnp.maximum(x, 0.0) + jnp.log1p(jnp.exp(-jnp.abs(x)))


def _split3(x):
    hi = x.astype(BF16)
    r1 = x - hi.astype(F32)
    mid = r1.astype(BF16)
    lo = (r1 - mid.astype(F32)).astype(BF16)
    return hi, mid, lo


def _mod_kernel(c_ref, w_ref, b_ref, o_ref):
    s = _silu(c_ref[...]).astype(BF16)
    o_ref[...] = jnp.dot(s, w_ref[...].astype(BF16), preferred_element_type=F32) + b_ref[...]


def _modulation(c, w_ada, b_ada):
    rows, d = c.shape
    n = w_ada.shape[1]
    bn = 1024
    return pl.pallas_call(
        _mod_kernel,
        grid=(n // bn,),
        in_specs=[
            pl.BlockSpec((rows, d), lambda j: (0, 0)),
            pl.BlockSpec((d, bn), lambda j: (0, j)),
            pl.BlockSpec((1, bn), lambda j: (0, j)),
        ],
        out_specs=pl.BlockSpec((rows, bn), lambda j: (0, j)),
        out_shape=jax.ShapeDtypeStruct((rows, n), F32),
        compiler_params=pltpu.CompilerParams(
            dimension_semantics=("arbitrary",), vmem_limit_bytes=VMEM_LIMIT),
        name="adaln_mod",
    )(c, w_ada, b_ada.reshape(1, n))


def _mix_kernel(*refs, T, ns, cl, pos_base, has_state):
    it = iter(refs)
    x_ref, sh_ref, sc_ref, pw_ref, win_ref, inv_ref = (next(it) for _ in range(6))
    cw_ref, cb_ref, dtb_ref, alog_ref, dsk_ref, nw_ref = (next(it) for _ in range(6))
    if has_state:
        sret_ref, sssm_ref, sconv_ref = (next(it) for _ in range(3))
    mix_ref, ret_ref, ssm_ref, conv_ref = (next(it) for _ in range(4))
    (hb_s, q_s, k_s, kd_s, v_s, gz_s, ext_s, act_s, dt_s, xc_s,
     mask_s, qdec_s, kdec_s) = (next(it) for _ in range(13))

    t_idx = pl.program_id(1)
    R = ns * T
    lg = [math.log(1.0 - 2.0 ** (-5.0 - h)) for h in range(RET_HEADS)]

    @pl.when(t_idx == 0)
    def _init():
        ii = lax.broadcasted_iota(jnp.int32, (T, T), 0)
        jj = lax.broadcasted_iota(jnp.int32, (T, T), 1)
        dist = jnp.abs(ii - jj).astype(F32)
        shift = int(math.log2(cl))
        visible = (jj >> shift) <= (ii >> shift)
        ri = lax.broadcasted_iota(jnp.int32, (T, LANES), 0).astype(F32)
        for h in range(RET_HEADS):
            mask_s[h] = jnp.where(visible, jnp.exp(lg[h] * dist), 0.0)
            qdec_s[h] = jnp.exp(lg[h] * (ri + 1.0))
            kdec_s[h] = jnp.exp(lg[h] * (T - 1.0 - ri))
        if has_state:
            ret_ref[...] = sret_ref[...]
            ssm_ref[...] = sssm_ref[...]
            ext_s[:, 0:CONV_PAD, :] = jnp.zeros((ns, CONV_PAD, CONV_DIM), F32)
            ext_s[:, CONV_PAD - (CONV_WIDTH - 1):CONV_PAD, :] = sconv_ref[...]
        else:
            ret_ref[...] = jnp.zeros(ret_ref.shape, F32)
            ssm_ref[...] = jnp.zeros(ssm_ref.shape, F32)
            ext_s[:, 0:CONV_PAD, :] = jnp.zeros((ns, CONV_PAD, CONV_DIM), F32)

    x3 = x_ref[...]
    ms = jnp.mean(x3 * x3, axis=-1, keepdims=True)
    hn = x3 * lax.rsqrt(ms + EPS) * pw_ref[...]
    hmod = hn * (1.0 + sc_ref[...]) + sh_ref[...]
    hb_s[...] = hmod.reshape(R, D_MODEL).astype(BF16)

    row = lax.broadcasted_iota(jnp.int32, (R, LANES), 0)
    pos = (pos_base + t_idx * T + (row & (T - 1))).astype(F32)
    ang = pos * inv_ref[...]
    lane = lax.broadcasted_iota(jnp.int32, (R, LANES), 1)
    cos_t = jnp.cos(ang)
    sin_t = jnp.where(lane < RET_DIM // 2, -jnp.sin(ang), jnp.sin(ang))

    def rot(u):
        return u * cos_t + pltpu.roll(u, RET_DIM // 2, 1) * sin_t

    def put(dst, c0, val):
        w = val.shape[1]
        for s in range(ns):
            dst[s, :, c0:c0 + w] = val[s * T:(s + 1) * T]

    hb = hb_s[...]
    for c0 in range(0, OFF_DT, PROJ_BLOCK):
        res = jnp.dot(hb, win_ref[:, c0:c0 + PROJ_BLOCK], preferred_element_type=F32)
        if c0 < OFF_V:
            for j in range(PROJ_BLOCK // RET_DIM):
                cj = c0 + j * RET_DIM
                r = rot(res[:, j * RET_DIM:(j + 1) * RET_DIM])
                if c0 < OFF_K:
                    put(q_s, cj, r.astype(BF16))
                else:
                    kf = r * (RET_DIM ** -0.5)
                    h = (cj - OFF_K) // RET_DIM
                    put(k_s, cj - OFF_K, kf.astype(BF16))
                    for s in range(ns):
                        kd_s[s, :, cj - OFF_K:cj - OFF_K + RET_DIM] = (
                            kf[s * T:(s + 1) * T] * kdec_s[h]).astype(BF16)
        elif c0 < OFF_G:
            put(v_s, c0 - OFF_V, res.astype(BF16))
        elif c0 < OFF_XBC:
            put(gz_s, c0 - OFF_G, _silu(res))
        else:
            for s in range(ns):
                ext_s[s, CONV_PAD:CONV_PAD + T, c0 - OFF_XBC:c0 - OFF_XBC + PROJ_BLOCK] = (
                    res[s * T:(s + 1) * T])
    res = jnp.dot(hb, win_ref[:, OFF_DT:OFF_DT + LANES], preferred_element_type=F32)
    put(dt_s, 0, _softplus(res + dtb_ref[...]))

    ti = lax.broadcasted_iota(jnp.int32, (T, T), 0)
    tj = lax.broadcasted_iota(jnp.int32, (T, T), 1)
    causal = tj <= ti
    tri = causal.astype(BF16)
    lane_t = lax.broadcasted_iota(jnp.int32, (T, LANES), 1)
    low_half = lane_t < SSM_HEADDIM
    a_row = -jnp.exp(alog_ref[...])

    def seq_body(s, carry):
        for h in range(RET_HEADS):
            cs = slice(h * RET_DIM, (h + 1) * RET_DIM)
            qh = q_s[s, :, cs]
            kh = k_s[s, :, cs]
            kdh = kd_s[s, :, cs]
            vh = v_s[s, :, cs]
            S_old = ret_ref[s, h]
            sc = lax.dot_general(qh, kh, (((1,), (1,)), ((), ())), preferred_element_type=F32)
            p = (sc * mask_s[h]).astype(BF16)
            o = jnp.dot(p, vh, preferred_element_type=F32)
            o = o + jnp.dot(qh, S_old.astype(BF16), preferred_element_type=F32) * qdec_s[h]
            upd = lax.dot_general(kdh, vh, (((0,), (0,)), ((), ())), preferred_element_type=F32)
            ret_ref[s, h] = math.exp(lg[h] * T) * S_old + upd
            mu = jnp.mean(o, axis=-1, keepdims=True)
            d = o - mu
            var = jnp.mean(d * d, axis=-1, keepdims=True)
            mix_ref[s, :, cs] = (d * lax.rsqrt(var + EPS) * gz_s[s, :, cs]).astype(BF16)

        for c0 in range(0, CONV_DIM, PROJ_BLOCK):
            cc = slice(c0, c0 + PROJ_BLOCK)
            acc = cb_ref[:, cc] + cw_ref[0:1, cc] * ext_s[s, CONV_PAD - 3:CONV_PAD - 3 + T, cc]
            for j in range(1, CONV_WIDTH):
                acc = acc + cw_ref[j:j + 1, cc] * ext_s[s, CONV_PAD - 3 + j:CONV_PAD - 3 + j + T, cc]
            act_s[:, cc] = _silu(acc)
        conv_ref[s] = ext_s[s, CONV_PAD + T - (CONV_WIDTH - 1):CONV_PAD + T, :]
        ext_s[s, 0:CONV_PAD, :] = ext_s[s, T:T + CONV_PAD, :]

        dt = dt_s[s]
        a = dt * a_row
        a_hi, a_mid, a_lo = _split3(a)
        cum = (jnp.dot(tri, a_hi, preferred_element_type=F32)
               + jnp.dot(tri, a_mid, preferred_element_type=F32)
               + jnp.dot(tri, a_lo, preferred_element_type=F32))
        cum_t = cum.T
        for g in range(SSM_GROUPS):
            b_g = act_s[:, SSM_WIDTH + g * SSM_STATE:SSM_WIDTH + (g + 1) * SSM_STATE].astype(BF16)
            c_g = act_s[:, SSM_WIDTH + (SSM_GROUPS + g) * SSM_STATE:
                        SSM_WIDTH + (SSM_GROUPS + g + 1) * SSM_STATE].astype(BF16)
            gmat = lax.dot_general(c_g, b_g, (((1,), (1,)), ((), ())), preferred_element_type=F32)
            rows_g = slice(g * HEADS_PER_GROUP * SSM_HEADDIM, (g + 1) * HEADS_PER_GROUP * SSM_HEADDIM)
            h_old = ssm_ref[s, rows_g, :]
            y_int = lax.dot_general(c_g, h_old.astype(BF16), (((1,), (1,)), ((), ())),
                                    preferred_element_type=F32)
            for m in range(HEADS_PER_GROUP // 2):
                pair = g * (HEADS_PER_GROUP // 2) + m
                h0 = 2 * pair
                pc = slice(pair * LANES, (pair + 1) * LANES)
                cb0 = jnp.broadcast_to(cum[:, h0:h0 + 1], (T, LANES))
                cb1 = jnp.broadcast_to(cum[:, h0 + 1:h0 + 2], (T, LANES))
                db0 = jnp.broadcast_to(dt[:, h0:h0 + 1], (T, LANES))
                db1 = jnp.broadcast_to(dt[:, h0 + 1:h0 + 2], (T, LANES))
                cum_p = jnp.where(low_half, cb0, cb1)
                dt_p = jnp.where(low_half, db0, db1)
                e_p = jnp.exp(cum_p)
                toend_p = jnp.exp(cum_p[T - 1:T, :] - cum_p)
                xs_p = act_s[:, pc]
                xdt = xs_p * dt_p
                xc_s[:, pc] = (xdt * toend_p).astype(BF16)
                ws = []
                for hh, cbh in ((h0, cb0), (h0 + 1, cb1)):
                    if T >= LANES:
                        colb = jnp.concatenate([cbh] * (T // LANES), axis=1)
                    else:
                        colb = cbh[:, :T]
                    seg = colb - cum_t[hh:hh + 1, :]
                    ws.append((jnp.where(causal, jnp.exp(jnp.where(causal, seg, 0.0)), 0.0)
                               * gmat).astype(BF16))
                xdt_b = xdt.astype(BF16)
                zero = jnp.zeros_like(xdt_b)
                rhs = jnp.concatenate([jnp.where(low_half, xdt_b, zero),
                                       jnp.where(low_half, zero, xdt_b)], axis=0)
                y = jnp.dot(jnp.concatenate(ws, axis=1), rhs, preferred_element_type=F32)
                y = y + y_int[:, m * LANES:(m + 1) * LANES] * e_p + dsk_ref[:, pc] * xs_p
                act_s[:, pc] = y * gz_s[s, :, RET_WIDTH + pair * LANES:RET_WIDTH + (pair + 1) * LANES]
            upd = lax.dot_general(xc_s[:, rows_g], b_g, (((0,), (0,)), ((), ())),
                                  preferred_element_type=F32)
            for hl in range(HEADS_PER_GROUP):
                hh = g * HEADS_PER_GROUP + hl
                dec = jnp.exp(cum_t[hh:hh + 1, T - 1:T])
                rr = slice(hh * SSM_HEADDIM, (hh + 1) * SSM_HEADDIM)
                ssm_ref[s, rr, :] = dec * ssm_ref[s, rr, :] + upd[hl * SSM_HEADDIM:(hl + 1) * SSM_HEADDIM]
        gw = SSM_WIDTH // SSM_GROUPS
        for g in range(SSM_GROUPS):
            gc = slice(g * gw, (g + 1) * gw)
            yz = act_s[:, gc]
            msq = jnp.mean(yz * yz, axis=-1, keepdims=True)
            mix_ref[s, :, RET_WIDTH + g * gw:RET_WIDTH + (g + 1) * gw] = (
                yz * lax.rsqrt(msq + EPS) * nw_ref[:, gc]).astype(BF16)
        return carry

    lax.fori_loop(0, ns, seq_body, 0)


def _mix_call(x, sh1, sc1, pre_w, win_b, inv2, conv_w, conv_b, dtb, alog, dskip_e, normw,
              states, *, T, ns, pos_base):
    B, L, D = x.shape
    has_state = states is not None
    cl = min(L, CHUNK)
    grid = (B // ns, L // T)
    R = ns * T

    def full(shape):
        nd = len(shape)
        return pl.BlockSpec(shape, lambda b, t, _nd=nd: (0,) * _nd, pipeline_mode=pl.Buffered(1))

    in_specs = [
        pl.BlockSpec((ns, T, D), lambda b, t: (b, t, 0)),
        pl.BlockSpec((ns, 1, D), lambda b, t: (b, 0, 0)),
        pl.BlockSpec((ns, 1, D), lambda b, t: (b, 0, 0)),
        full((1, D)),
        full(win_b.shape),
        full((1, LANES)),
        full(conv_w.shape),
        full((1, CONV_DIM)),
        full((1, LANES)),
        full((1, LANES)),
        full((1, SSM_WIDTH)),
        full((1, SSM_WIDTH)),
    ]
    args = [x, sh1, sc1, pre_w, win_b, inv2, conv_w, conv_b, dtb, alog, dskip_e, normw]
    if has_state:
        in_specs += [
            pl.BlockSpec((ns, RET_HEADS, RET_DIM, RET_DIM), lambda b, t: (b, 0, 0, 0)),
            pl.BlockSpec((ns, SSM_WIDTH, SSM_STATE), lambda b, t: (b, 0, 0)),
            pl.BlockSpec((ns, CONV_WIDTH - 1, CONV_DIM), lambda b, t: (b, 0, 0)),
        ]
        args += list(states)
    out_specs = [
        pl.BlockSpec((ns, T, D), lambda b, t: (b, t, 0)),
        pl.BlockSpec((ns, RET_HEADS, RET_DIM, RET_DIM), lambda b, t: (b, 0, 0, 0)),
        pl.BlockSpec((ns, SSM_WIDTH, SSM_STATE), lambda b, t: (b, 0, 0)),
        pl.BlockSpec((ns, CONV_WIDTH - 1, CONV_DIM), lambda b, t: (b, 0, 0)),
    ]
    out_shape = [
        jax.ShapeDtypeStruct((B, L, D), BF16),
        jax.ShapeDtypeStruct((B, RET_HEADS, RET_DIM, RET_DIM), F32),
        jax.ShapeDtypeStruct((B, SSM_WIDTH, SSM_STATE), F32),
        jax.ShapeDtypeStruct((B, CONV_WIDTH - 1, CONV_DIM), F32),
    ]
    scratch = [
        pltpu.VMEM((R, D), BF16),
        pltpu.VMEM((ns, T, RET_WIDTH), BF16),
        pltpu.VMEM((ns, T, RET_WIDTH), BF16),
        pltpu.VMEM((ns, T, RET_WIDTH), BF16),
        pltpu.VMEM((ns, T, RET_WIDTH), BF16),
        pltpu.VMEM((ns, T, 2 * RET_WIDTH), F32),
        pltpu.VMEM((ns, T + CONV_PAD, CONV_DIM), F32),
        pltpu.VMEM((T, CONV_DIM), F32),
        pltpu.VMEM((ns, T, LANES), F32),
        pltpu.VMEM((T, SSM_WIDTH), BF16),
        pltpu.VMEM((RET_HEADS, T, T), F32),
        pltpu.VMEM((RET_HEADS, T, LANES), F32),
        pltpu.VMEM((RET_HEADS, T, LANES), F32),
    ]
    kern = functools.partial(_mix_kernel, T=T, ns=ns, cl=cl, pos_base=pos_base, has_state=has_state)
    return pl.pallas_call(
        kern,
        grid=grid,
        in_specs=in_specs,
        out_specs=out_specs,
        out_shape=out_shape,
        scratch_shapes=scratch,
        compiler_params=pltpu.CompilerParams(
            dimension_semantics=("arbitrary", "arbitrary"), vmem_limit_bytes=VMEM_LIMIT),
        name="token_mix_state" if has_state else "token_mix",
    )(*args)


def _ffn_kernel(x_ref, mix_ref, g1_ref, sh_ref, sc_ref, g2_ref, wout_ref, pmw_ref, pfw_ref, qfw_ref,
                wup_ref, wdn_ref, o_ref, x1_s, h2_s, acc_s, *, ns, T):
    f = pl.program_id(1)
    R = ns * T

    def rms(v, w):
        return v * lax.rsqrt(jnp.mean(v * v, axis=-1, keepdims=True) + EPS) * w

    @pl.when(f == 0)
    def _first():
        m = jnp.dot(mix_ref[...].reshape(R, D_MODEL), wout_ref[...], preferred_element_type=F32)
        m3 = rms(m, pmw_ref[...]).reshape(ns, T, D_MODEL)
        x1 = x_ref[...] + g1_ref[...] * m3
        x1_s[...] = x1
        h2 = rms(x1, pfw_ref[...]) * (1.0 + sc_ref[...]) + sh_ref[...]
        h2_s[...] = h2.reshape(R, D_MODEL).astype(BF16)

    u = jnp.dot(h2_s[...], wup_ref[...], preferred_element_type=F32)
    u = jnp.maximum(u, 0.0)
    u = (u * u).astype(BF16)
    part = jnp.dot(u, wdn_ref[...], preferred_element_type=F32)

    @pl.when(f == 0)
    def _set():
        acc_s[...] = part

    @pl.when(f != 0)
    def _add():
        acc_s[...] += part

    @pl.when(f == pl.num_programs(1) - 1)
    def _last():
        y3 = rms(acc_s[...], qfw_ref[...]).reshape(ns, T, D_MODEL)
        o_ref[...] = x1_s[...] + g2_ref[...] * y3


def _ffn_call(x, mix, g1, sh2, sc2, g2, wout_b, post_mix_w, pre_ffn_w, post_ffn_w, wup_b, wdn_b,
              *, T, ns, fc):
    B, L, D = x.shape
    grid = (B // ns * (L // T), D_FF // fc)
    nt = L // T

    def tok(i, f):
        return (i // nt, i % nt, 0)

    def seq(i, f):
        return (i // nt, 0, 0)

    def full(shape):
        return pl.BlockSpec(shape, lambda i, f: (0, 0), pipeline_mode=pl.Buffered(1))

    kern = functools.partial(_ffn_kernel, ns=ns, T=T)
    return pl.pallas_call(
        kern,
        grid=grid,
        in_specs=[
            pl.BlockSpec((ns, T, D), tok),
            pl.BlockSpec((ns, T, D), tok),
            pl.BlockSpec((ns, 1, D), seq),
            pl.BlockSpec((ns, 1, D), seq),
            pl.BlockSpec((ns, 1, D), seq),
            pl.BlockSpec((ns, 1, D), seq),
            full((D, D)),
            full((1, D)),
            full((1, D)),
            full((1, D)),
            pl.BlockSpec((D, fc), lambda i, f: (0, f)),
            pl.BlockSpec((fc, D), lambda i, f: (f, 0)),
        ],
        out_specs=pl.BlockSpec((ns, T, D), tok),
        out_shape=jax.ShapeDtypeStruct((B, L, D), F32),
        scratch_shapes=[
            pltpu.VMEM((ns, T, D), F32),
            pltpu.VMEM((ns * T, D), BF16),
            pltpu.VMEM((ns * T, D), F32),
        ],
        compiler_params=pltpu.CompilerParams(
            dimension_semantics=("arbitrary", "arbitrary"), vmem_limit_bytes=VMEM_LIMIT),
        name="channel_mix",
    )(x, mix, g1, sh2, sc2, g2, wout_b, post_mix_w, pre_ffn_w, post_ffn_w, wup_b, wdn_b)


def _pick(n, pref):
    t = min(n, pref)
    while n % t:
        t //= 2
    return t


def _layer(x, mod, states, lw, *, pos_base, mix_T, mix_ns, ffn_T, ffn_ns):
    (pre_mix_w, post_mix_w, pre_ffn_w, post_ffn_w, win_b, conv_w, conv_b, dtb, alog, dskip_e,
     normw, wout_b, wup_b, wdn_b, inv2) = lw
    B = x.shape[0]
    sh1, sc1, g1, sh2, sc2, g2 = [m.reshape(B, 1, D_MODEL) for m in jnp.split(mod, 6, axis=-1)]
    mix, r_new, s_new, c_new = _mix_call(
        x, sh1, sc1, pre_mix_w, win_b, inv2, conv_w, conv_b, dtb, alog, dskip_e, normw, states,
        T=mix_T, ns=mix_ns, pos_base=pos_base)
    y = _ffn_call(x, mix, g1, sh2, sc2, g2, wout_b, post_mix_w, pre_ffn_w, post_ffn_w, wup_b, wdn_b,
                  T=ffn_T, ns=ffn_ns, fc=512)
    return y, r_new, s_new.reshape(B, SSM_HEADS, SSM_HEADDIM, SSM_STATE), c_new


def kernel(x_prompt, x_sample, state_ret, state_ssm, state_conv, c_prompt, c_sample, w_ada, b_ada,
           pre_mix_w, post_mix_w, pre_ffn_w, post_ffn_w, w_in, conv_w, conv_b, dt_bias, a_log, d_skip,
           ssm_norm_w, w_out, w_up, w_down):
    depth = w_ada.shape[0]
    bp, lp, _ = x_prompt.shape
    bs, ls, _ = x_sample.shape
    half = RET_DIM // 2
    inv = ROPE_BASE ** (-jnp.arange(half, dtype=F32) / half)
    inv2 = jnp.concatenate([inv, inv]).reshape(1, RET_DIM)

    def pad_lanes(v):
        return jnp.pad(v.astype(F32), (0, LANES - v.shape[0])).reshape(1, LANES)

    yp, ys = x_prompt, x_sample
    outs = [[] for _ in range(6)]
    rows = bp + bs
    rows_pad = -(-rows // 8) * 8
    for l in range(depth):
        c_all = jnp.pad(jnp.concatenate([c_prompt, c_sample], axis=0), ((0, rows_pad - rows), (0, 0)))
        mod = _modulation(c_all, w_ada[l], b_ada[l])
        win_b = jnp.pad(w_in[l], ((0, 0), (0, IN_COLS_PAD - IN_COLS))).astype(BF16)
        lw = (pre_mix_w[l].reshape(1, -1), post_mix_w[l].reshape(1, -1), pre_ffn_w[l].reshape(1, -1),
              post_ffn_w[l].reshape(1, -1), win_b, conv_w[l], conv_b[l].reshape(1, -1),
              pad_lanes(dt_bias[l]), pad_lanes(a_log[l]),
              jnp.repeat(d_skip[l].astype(F32), SSM_HEADDIM).reshape(1, -1),
              ssm_norm_w[l].reshape(1, -1), w_out[l].astype(BF16), w_up[l].astype(BF16),
              w_down[l].astype(BF16), inv2)
        yp, r, s, c = _layer(yp, mod[:bp], None, lw, pos_base=0,
                             mix_T=_pick(lp, 256), mix_ns=1, ffn_T=_pick(lp, 512), ffn_ns=1)
        outs[0].append(r); outs[1].append(s); outs[2].append(c)
        ns_mix = _pick(bs, max(1, 128 // ls))
        ns_ffn = _pick(bs, max(1, 512 // ls))
        st = (state_ret[l], state_ssm[l].reshape(bs, SSM_WIDTH, SSM_STATE), state_conv[l])
        ys, r, s, c = _layer(ys, mod[bp:bp + bs], st, lw, pos_base=PAST_LEN,
                             mix_T=ls, mix_ns=ns_mix, ffn_T=ls, ffn_ns=ns_ffn)
        outs[3].append(r); outs[4].append(s); outs[5].append(c)
    return (yp, ys) + tuple(jnp.stack(o) for o in outs)
```

```python
import functools
import math

import jax
import jax.numpy as jnp
from jax import lax
from jax.experimental import pallas as pl
from jax.experimental.pallas import tpu as pltpu

F32 = jnp.float32
BF16 = jnp.bfloat16

D_MODEL = 2048
PAST_LEN = 4096
CHUNK = 64
RET_HEADS = 8
RET_DIM = 128
RET_WIDTH = RET_HEADS * RET_DIM
SSM_HEADS = 16
SSM_HEADDIM = 64
SSM_WIDTH = SSM_HEADS * SSM_HEADDIM
SSM_GROUPS = 2
HEADS_PER_GROUP = SSM_HEADS // SSM_GROUPS
SSM_STATE = 128
CONV_WIDTH = 4
CONV_DIM = SSM_WIDTH + 2 * SSM_GROUPS * SSM_STATE
D_FF = 4 * D_MODEL
ROPE_BASE = 10000.0
EPS = 1e-6

LANES = 128
CONV_PAD = 8
OFF_Q = 0
OFF_K = RET_WIDTH
OFF_V = 2 * RET_WIDTH
OFF_G = 3 * RET_WIDTH
OFF_Z = 4 * RET_WIDTH
OFF_XBC = 4 * RET_WIDTH + SSM_WIDTH
OFF_DT = OFF_XBC + CONV_DIM
IN_COLS = OFF_DT + SSM_HEADS
IN_COLS_PAD = OFF_DT + LANES
PROJ_BLOCK = 512

VMEM_LIMIT = 60 * 1024 * 1024


def _silu(x):
    return x * (1.0 / (1.0 + jnp.exp(-x)))


def _softplus(x):
    return jnp.maximum(x, 0.0) + jnp.log1p(jnp.exp(-jnp.abs(x)))


def _split3(x):
    hi = x.astype(BF16)
    r1 = x - hi.astype(F32)
    mid = r1.astype(BF16)
    lo = (r1 - mid.astype(F32)).astype(BF16)
    return hi, mid, lo


def _mod_kernel(c_ref, w_ref, b_ref, o_ref):
    s = _silu(c_ref[...]).astype(BF16)
    o_ref[...] = jnp.dot(s, w_ref[...].astype(BF16), preferred_element_type=F32) + b_ref[...]


def _modulation(c, w_ada, b_ada):
    rows, d = c.shape
    n = w_ada.shape[1]
    bn = 1024
    return pl.pallas_call(
        _mod_kernel,
        grid=(n // bn,),
        in_specs=[
            pl.BlockSpec((rows, d), lambda j: (0, 0)),
            pl.BlockSpec((d, bn), lambda j: (0, j)),
            pl.BlockSpec((1, bn), lambda j: (0, j)),
        ],
        out_specs=pl.BlockSpec((rows, bn), lambda j: (0, j)),
        out_shape=jax.ShapeDtypeStruct((rows, n), F32),
        compiler_params=pltpu.CompilerParams(
            dimension_semantics=("arbitrary",), vmem_limit_bytes=VMEM_LIMIT),
        name="adaln_mod",
    )(c, w_ada, b_ada.reshape(1, n))


def _mix_kernel(*refs, T, ns, cl, pos_base, has_state, pipelined):
    it = iter(refs)
    x_ref, sh_ref, sc_ref, pw_ref, win_ref, inv_ref = (next(it) for _ in range(6))
    cw_ref, cb_ref, dtb_ref, alog_ref, dsk_ref, nw_ref = (next(it) for _ in range(6))
    if has_state:
        sret_ref, sssm_ref, sconv_ref = (next(it) for _ in range(3))
    mix_ref, ret_ref, ssm_ref, conv_ref = (next(it) for _ in range(4))
    (hb_s, q_s, k_s, kd_s, v_s, gz_s, ext_s, act_s, dt_s, xc_s,
     mask_s, qdec_s, kdec_s) = (next(it) for _ in range(13))

    b_idx = pl.program_id(0)
    t_idx = pl.program_id(1)
    R = ns * T
    lg = [math.log(1.0 - 2.0 ** (-5.0 - h)) for h in range(RET_HEADS)]
    if pipelined:
        sa = t_idx & 1
        sb = 1 - sa
    else:
        sa = sb = 0

    @pl.when((b_idx == 0) & (t_idx == 0))
    def _tables():
        ii = lax.broadcasted_iota(jnp.int32, (T, T), 0)
        jj = lax.broadcasted_iota(jnp.int32, (T, T), 1)
        dist = jnp.abs(ii - jj).astype(F32)
        shift = int(math.log2(cl))
        visible = (jj >> shift) <= (ii >> shift)
        ri = lax.broadcasted_iota(jnp.int32, (T, LANES), 0).astype(F32)
        for h in range(RET_HEADS):
            mask_s[h] = jnp.where(visible, jnp.exp(lg[h] * dist), 0.0)
            qdec_s[h] = jnp.exp(lg[h] * (ri + 1.0))
            kdec_s[h] = jnp.exp(lg[h] * (T - 1.0 - ri))
        if pipelined:
            for buf in (q_s, k_s, kd_s, v_s, gz_s, ext_s, dt_s):
                buf[1] = jnp.zeros(buf.shape[1:], buf.dtype)

    @pl.when(t_idx <= (1 if pipelined else 0))
    def _init():
        if has_state:
            ret_ref[...] = sret_ref[...]
            ssm_ref[...] = sssm_ref[...]
        else:
            ret_ref[...] = jnp.zeros(ret_ref.shape, F32)
            ssm_ref[...] = jnp.zeros(ssm_ref.shape, F32)
        for slot in range(ext_s.shape[0]):
            ext_s[slot, :, 0:CONV_PAD, :] = jnp.zeros((ns, CONV_PAD, CONV_DIM), F32)
            if has_state:
                ext_s[slot, :, CONV_PAD - (CONV_WIDTH - 1):CONV_PAD, :] = sconv_ref[...]

    pj = {}

    def proj_prologue():
        scale = pw_ref[...] * (1.0 + sc_ref[...])
        rc = min(T, 64)
        for r0 in range(0, T, rc):
            x3 = x_ref[:, r0:r0 + rc, :]
            ms = jnp.mean(x3 * x3, axis=-1, keepdims=True)
            hmod = x3 * lax.rsqrt(ms + EPS) * scale + sh_ref[...]
            for s in range(ns):
                hb_s[s * T + r0:s * T + r0 + rc, :] = hmod[s].astype(BF16)
        row =lax.broadcasted_iota(jnp.int32, (R, LANES), 0)
        pos = (pos_base + t_idx * T + (row & (T - 1))).astype(F32)
        ang = pos * inv_ref[...]
        lane = lax.broadcasted_iota(jnp.int32, (R, LANES), 1)
        pj["cos"] = jnp.cos(ang)
        pj["sin"] = jnp.where(lane < RET_DIM // 2, -jnp.sin(ang), jnp.sin(ang))

    def rot(u):
        return u * pj["cos"] + pltpu.roll(u, RET_DIM // 2, 1) * pj["sin"]

    def put(dst, c0, val):
        w = val.shape[1]
        for s in range(ns):
            dst[sa, s, :, c0:c0 + w] = val[s * T:(s + 1) * T]

    def proj_block(c0):
        res = jnp.dot(hb_s[...], win_ref[:, c0:c0 + PROJ_BLOCK], preferred_element_type=F32)
        if c0 < OFF_V:
            for j in range(PROJ_BLOCK // RET_DIM):
                cj = c0 + j * RET_DIM
                r = rot(res[:, j * RET_DIM:(j + 1) * RET_DIM])
                if c0 < OFF_K:
                    put(q_s, cj, r.astype(BF16))
                else:
                    kf = r * (RET_DIM ** -0.5)
                    h = (cj - OFF_K) // RET_DIM
                    put(k_s, cj - OFF_K, kf.astype(BF16))
                    for s in range(ns):
                        kd_s[sa, s, :, cj - OFF_K:cj - OFF_K + RET_DIM] = (
                            kf[s * T:(s + 1) * T] * kdec_s[h]).astype(BF16)
        elif c0 < OFF_G:
            put(v_s, c0 - OFF_V, res.astype(BF16))
        elif c0 < OFF_XBC:
            put(gz_s, c0 - OFF_G, _silu(res))
        else:
            for s in range(ns):
                ext_s[sa, s, CONV_PAD:CONV_PAD + T, c0 - OFF_XBC:c0 - OFF_XBC + PROJ_BLOCK] = (
                    res[s * T:(s + 1) * T])

    def proj_dt():
        res = jnp.dot(hb_s[...], win_ref[:, OFF_DT:OFF_DT + LANES], preferred_element_type=F32)
        put(dt_s, 0, _softplus(res + dtb_ref[...]))

    proj_units = [proj_prologue]
    proj_units += [functools.partial(proj_block, c0) for c0 in range(0, OFF_DT, PROJ_BLOCK)]
    proj_units += [proj_dt]

    def mixer_units(s):
        mx = {}
        units = []

        def consts():
            ti = lax.broadcasted_iota(jnp.int32, (T, T), 0)
            tj = lax.broadcasted_iota(jnp.int32, (T, T), 1)
            mx["causal"] = tj <= ti
            mx["tri"] = mx["causal"].astype(BF16)
            mx["low"] = lax.broadcasted_iota(jnp.int32, (T, LANES), 1) < SSM_HEADDIM
        units.append(consts)

        def ret_head(h):
            cs = slice(h * RET_DIM, (h + 1) * RET_DIM)
            qh = q_s[sb, s, :, cs]
            kh = k_s[sb, s, :, cs]
            kdh = kd_s[sb, s, :, cs]
            vh = v_s[sb, s, :, cs]
            S_old = ret_ref[s, h]
            sc = lax.dot_general(qh, kh, (((1,), (1,)), ((), ())), preferred_element_type=F32)
            p = (sc * mask_s[h]).astype(BF16)
            o = jnp.dot(p, vh, preferred_element_type=F32)
            o = o + jnp.dot(qh, S_old.astype(BF16), preferred_element_type=F32) * qdec_s[h]
            upd = lax.dot_general(kdh, vh, (((0,), (0,)), ((), ())), preferred_element_type=F32)
            ret_ref[s, h] = math.exp(lg[h] * T) * S_old + upd
            mu = jnp.mean(o, axis=-1, keepdims=True)
            d = o - mu
            var = jnp.mean(d * d, axis=-1, keepdims=True)
            mix_ref[s, :, cs] = (d * lax.rsqrt(var + EPS) * gz_s[sb, s, :, cs]).astype(BF16)
        units += [functools.partial(ret_head, h) for h in range(RET_HEADS)]

        def conv_block(c0):
            cc = slice(c0, c0 + PROJ_BLOCK)
            base = CONV_PAD - (CONV_WIDTH - 1)
            acc = cb_ref[:, cc] + cw_ref[0:1, cc] * ext_s[sb, s, base:base + T, cc]
            for j in range(1, CONV_WIDTH):
                acc = acc + cw_ref[j:j + 1, cc] * ext_s[sb, s, base + j:base + j + T, cc]
            act_s[:, cc] = _silu(acc)
        units += [functools.partial(conv_block, c0) for c0 in range(0, CONV_DIM, PROJ_BLOCK)]

        def conv_tail():
            conv_ref[s] = ext_s[sb, s, CONV_PAD + T - (CONV_WIDTH - 1):CONV_PAD + T, :]
            ext_s[sa, s, 0:CONV_PAD, :] = ext_s[sb, s, T:T + CONV_PAD, :]

        def ssd_cum():
            conv_tail()
            dt = dt_s[sb, s]
            a = dt * (-jnp.exp(alog_ref[...]))
            a_hi, a_mid, a_lo = _split3(a)
            tri = mx["tri"]
            cum = (jnp.dot(tri, a_hi, preferred_element_type=F32)
                   + jnp.dot(tri, a_mid, preferred_element_type=F32)
                   + jnp.dot(tri, a_lo, preferred_element_type=F32))
            mx["dt"] = dt
            mx["cum"] = cum
            mx["cum_t"] = cum.T
        units.append(ssd_cum)

        def grp_rows(g):
            return slice(g * HEADS_PER_GROUP * SSM_HEADDIM, (g + 1) * HEADS_PER_GROUP * SSM_HEADDIM)

        def ssd_group(g):
            b_g = act_s[:, SSM_WIDTH + g * SSM_STATE:SSM_WIDTH + (g + 1) * SSM_STATE].astype(BF16)
            c_g = act_s[:, SSM_WIDTH + (SSM_GROUPS + g) * SSM_STATE:
                        SSM_WIDTH + (SSM_GROUPS + g + 1) * SSM_STATE].astype(BF16)
            mx["b_g"] = b_g
            mx["gmat"] = lax.dot_general(c_g, b_g, (((1,), (1,)), ((), ())),
                                         preferred_element_type=F32)
            h_old = ssm_ref[s, grp_rows(g), :]
            mx["y_int"] = lax.dot_general(c_g, h_old.astype(BF16), (((1,), (1,)), ((), ())),
                                          preferred_element_type=F32)

        def ssd_pair(g, m):
            causal, low = mx["causal"], mx["low"]
            cum, cum_t, dt = mx["cum"], mx["cum_t"], mx["dt"]
            pair = g * (HEADS_PER_GROUP // 2) + m
            h0 = 2 * pair
            pc = slice(pair * LANES, (pair + 1) * LANES)
            cb0 = jnp.broadcast_to(cum[:, h0:h0 + 1], (T, LANES))
            cb1 = jnp.broadcast_to(cum[:, h0 + 1:h0 + 2], (T, LANES))
            db0 = jnp.broadcast_to(dt[:, h0:h0 + 1], (T, LANES))
            db1 = jnp.broadcast_to(dt[:, h0 + 1:h0 + 2], (T, LANES))
            cum_p = jnp.where(low, cb0, cb1)
            dt_p = jnp.where(low, db0, db1)
            e_p = jnp.exp(cum_p)
            toend_p = jnp.exp(cum_p[T - 1:T, :] - cum_p)
            xs_p = act_s[:, pc]
            xdt = xs_p * dt_p
            xc_s[:, pc] = (xdt * toend_p).astype(BF16)
            ws = []
            for hh, cbh in ((h0, cb0), (h0 + 1, cb1)):
                if T >= LANES:
                    colb = jnp.concatenate([cbh] * (T // LANES), axis=1)
                else:
                    colb = cbh[:, :T]
                seg = colb - cum_t[hh:hh + 1, :]
                ws.append((jnp.where(causal, jnp.exp(jnp.where(causal, seg, 0.0)), 0.0)
                           * mx["gmat"]).astype(BF16))
            xdt_b = xdt.astype(BF16)
            zero = jnp.zeros_like(xdt_b)
            rhs = jnp.concatenate([jnp.where(low, xdt_b, zero),
                                   jnp.where(low, zero, xdt_b)], axis=0)
            y = jnp.dot(jnp.concatenate(ws, axis=1), rhs, preferred_element_type=F32)
            y = y + mx["y_int"][:, m * LANES:(m + 1) * LANES] * e_p + dsk_ref[:, pc] * xs_p
            act_s[:, pc] = y * gz_s[sb, s, :, RET_WIDTH + pair * LANES:RET_WIDTH + (pair + 1) * LANES]

        def ssd_state(g):
            upd = lax.dot_general(xc_s[:, grp_rows(g)], mx["b_g"], (((0,), (0,)), ((), ())),
                                  preferred_element_type=F32)
            for hl in range(HEADS_PER_GROUP):
                hh = g * HEADS_PER_GROUP + hl
                dec = jnp.exp(mx["cum_t"][hh:hh + 1, T - 1:T])
                rr = slice(hh * SSM_HEADDIM, (hh + 1) * SSM_HEADDIM)
                ssm_ref[s, rr, :] = dec * ssm_ref[s, rr, :] + upd[hl * SSM_HEADDIM:(hl + 1) * SSM_HEADDIM]

        for g in range(SSM_GROUPS):
            units.append(functools.partial(ssd_group, g))
            units += [functools.partial(ssd_pair, g, m) for m in range(HEADS_PER_GROUP // 2)]
            units.append(functools.partial(ssd_state, g))

        def ssd_norm(g):
            gw = SSM_WIDTH // SSM_GROUPS
            gc = slice(g * gw, (g + 1) * gw)
            yz = act_s[:, gc]
            msq = jnp.mean(yz * yz, axis=-1, keepdims=True)
            mix_ref[s, :, RET_WIDTH + g * gw:RET_WIDTH + (g + 1) * gw] = (
                yz * lax.rsqrt(msq + EPS) * nw_ref[:, gc]).astype(BF16)
        units += [functools.partial(ssd_norm, g) for g in range(SSM_GROUPS)]
        return units

    if pipelined:
        a_units, b_units = proj_units, mixer_units(0)
        plan = [[0, 1, 2], [9], [10], [11], [12, 13], [14], [15], [16], [17, 18], [19, 20], [21],
                [22], [23, 24], [25, 26, 3], [4, 5, 6, 7, 8]]
        assert len(plan) == len(a_units) and sorted(sum(plan, [])) == list(range(len(b_units)))
        for au, bis in zip(a_units, plan):
            au()
            for bi in bis:
                b_units[bi]()
    else:
        for au in proj_units:
            au()

        def seq_body(s, carry):
            for bu in mixer_units(s):
                bu()
            return carry

        if ns == 1:
            seq_body(0, 0)
        else:
            lax.fori_loop(0, ns, seq_body, 0)


def _mix_call(x, sh1, sc1, pre_w, win_b, inv2, conv_w, conv_b, dtb, alog, dskip_e, normw,
              states, *, T, ns, pos_base, pipelined):
    B, L, D = x.shape
    has_state = states is not None
    cl = min(L, CHUNK)
    n_t = L // T
    lag = 1 if pipelined else 0
    assert not pipelined or ns == 1
    grid = (B // ns, n_t + lag)
    R = ns * T
    nslot = 2 if pipelined else 1

    def full(shape):
        nd = len(shape)
        return pl.BlockSpec(shape, lambda b, t, _nd=nd: (0,) * _nd, pipeline_mode=pl.Buffered(1))

    in_specs = [
        pl.BlockSpec((ns, T, D), lambda b, t: (b, jnp.minimum(t, n_t - 1), 0)),
        pl.BlockSpec((ns, 1, D), lambda b, t: (b, 0, 0)),
        pl.BlockSpec((ns, 1, D), lambda b, t: (b, 0, 0)),
        full((1, D)),
        full(win_b.shape),
        full((1, LANES)),
        full(conv_w.shape),
        full((1, CONV_DIM)),
        full((1, LANES)),
        full((1, LANES)),
        full((1, SSM_WIDTH)),
        full((1, SSM_WIDTH)),
    ]
    args = [x, sh1, sc1, pre_w, win_b, inv2, conv_w, conv_b, dtb, alog, dskip_e, normw]
    if has_state:
        in_specs += [
            pl.BlockSpec((ns, RET_HEADS, RET_DIM, RET_DIM), lambda b, t: (b, 0, 0, 0)),
            pl.BlockSpec((ns, SSM_WIDTH, SSM_STATE), lambda b, t: (b, 0, 0)),
            pl.BlockSpec((ns, CONV_WIDTH - 1, CONV_DIM), lambda b, t: (b, 0, 0)),
        ]
        args += list(states)
    out_specs = [
        pl.BlockSpec((ns, T, D), lambda b, t: (b, jnp.maximum(t - lag, 0), 0)),
        pl.BlockSpec((ns, RET_HEADS, RET_DIM, RET_DIM), lambda b, t: (b, 0, 0, 0)),
        pl.BlockSpec((ns, SSM_WIDTH, SSM_STATE), lambda b, t: (b, 0, 0)),
        pl.BlockSpec((ns, CONV_WIDTH - 1, CONV_DIM), lambda b, t: (b, 0, 0)),
    ]
    out_shape = [
        jax.ShapeDtypeStruct((B, L, D), BF16),
        jax.ShapeDtypeStruct((B, RET_HEADS, RET_DIM, RET_DIM), F32),
        jax.ShapeDtypeStruct((B, SSM_WIDTH, SSM_STATE), F32),
        jax.ShapeDtypeStruct((B, CONV_WIDTH - 1, CONV_DIM), F32),
    ]
    scratch = [
        pltpu.VMEM((R, D), BF16),
        pltpu.VMEM((nslot, ns, T, RET_WIDTH), BF16),
        pltpu.VMEM((nslot, ns, T, RET_WIDTH), BF16),
        pltpu.VMEM((nslot, ns, T, RET_WIDTH), BF16),
        pltpu.VMEM((nslot, ns, T, RET_WIDTH), BF16),
        pltpu.VMEM((nslot, ns, T, 2 * RET_WIDTH), F32),
        pltpu.VMEM((nslot, ns, T + CONV_PAD, CONV_DIM), F32),
        pltpu.VMEM((T, CONV_DIM), F32),
        pltpu.VMEM((nslot, ns, T, LANES), F32),
        pltpu.VMEM((T, SSM_WIDTH), BF16),
        pltpu.VMEM((RET_HEADS, T, T), F32),
        pltpu.VMEM((RET_HEADS, T, LANES), F32),
        pltpu.VMEM((RET_HEADS, T, LANES), F32),
    ]
    kern = functools.partial(_mix_kernel, T=T, ns=ns, cl=cl, pos_base=pos_base, has_state=has_state,
                             pipelined=pipelined)
    return pl.pallas_call(
        kern,
        grid=grid,
        in_specs=in_specs,
        out_specs=out_specs,
        out_shape=out_shape,
        scratch_shapes=scratch,
        compiler_params=pltpu.CompilerParams(
            dimension_semantics=("arbitrary", "arbitrary"), vmem_limit_bytes=VMEM_LIMIT),
        name="token_mix_state" if has_state else "token_mix",
    )(*args)


def _ffn_kernel(x_ref, mix_ref, g1_ref, sh_ref, sc_ref, g2_ref, wout_ref, pmw_ref, pfw_ref, qfw_ref,
                wup_ref, wdn_ref, o_ref, x1_s, h2_s, acc_s, *, ns, T):
    f = pl.program_id(1)
    R = ns * T

    def rms(v, w):
        return v * lax.rsqrt(jnp.mean(v * v, axis=-1, keepdims=True) + EPS) * w

    @pl.when(f == 0)
    def _first():
        m = jnp.dot(mix_ref[...].reshape(R, D_MODEL), wout_ref[...], preferred_element_type=F32)
        m3 = rms(m, pmw_ref[...]).reshape(ns, T, D_MODEL)
        x1 = x_ref[...] + g1_ref[...] * m3
        x1_s[...] = x1
        h2 = rms(x1, pfw_ref[...]) * (1.0 + sc_ref[...]) + sh_ref[...]
        h2_s[...] = h2.reshape(R, D_MODEL).astype(BF16)
        acc_s[...] = jnp.zeros(acc_s.shape, F32)

    u = jnp.dot(h2_s[...], wup_ref[...], preferred_element_type=F32)
    u = jnp.maximum(u, 0.0)
    u = (u * u).astype(BF16)
    acc_s[...] += jnp.dot(u, wdn_ref[...], preferred_element_type=F32)

    @pl.when(f == pl.num_programs(1) - 1)
    def _last():
        y3 = rms(acc_s[...], qfw_ref[...]).reshape(ns, T, D_MODEL)
        o_ref[...] = x1_s[...] + g2_ref[...] * y3


def _ffn_call(x, mix, g1, sh2, sc2, g2, wout_b, post_mix_w, pre_ffn_w, post_ffn_w, wup_b, wdn_b,
              *, T, ns, fc):
    B, L, D = x.shape
    grid = (B // ns * (L // T), D_FF // fc)
    nt = L // T

    def tok(i, f):
        return (i // nt, i % nt, 0)

    def seq(i, f):
        return (i // nt, 0, 0)

    def full(shape):
        return pl.BlockSpec(shape, lambda i, f: (0, 0), pipeline_mode=pl.Buffered(1))

    kern = functools.partial(_ffn_kernel, ns=ns, T=T)
    return pl.pallas_call(
        kern,
        grid=grid,
        in_specs=[
            pl.BlockSpec((ns, T, D), tok),
            pl.BlockSpec((ns, T, D), tok),
            pl.BlockSpec((ns, 1, D), seq),
            pl.BlockSpec((ns, 1, D), seq),
            pl.BlockSpec((ns, 1, D), seq),
            pl.BlockSpec((ns, 1, D), seq),
            full((D, D)),
            full((1, D)),
            full((1, D)),
            full((1, D)),
            pl.BlockSpec((D, fc), lambda i, f: (0, f)),
            pl.BlockSpec((fc, D), lambda i, f: (f, 0)),
        ],
        out_specs=pl.BlockSpec((ns, T, D), tok),
        out_shape=jax.ShapeDtypeStruct((B, L, D), F32),
        scratch_shapes=[
            pltpu.VMEM((ns, T, D), F32),
            pltpu.VMEM((ns * T, D), BF16),
            pltpu.VMEM((ns * T, D), F32),
        ],
        compiler_params=pltpu.CompilerParams(
            dimension_semantics=("arbitrary", "arbitrary"), vmem_limit_bytes=VMEM_LIMIT),
        name="channel_mix",
    )(x, mix, g1, sh2, sc2, g2, wout_b, post_mix_w, pre_ffn_w, post_ffn_w, wup_b, wdn_b)


def _pick(n, pref):
    t = min(n, pref)
    while n % t:
        t //= 2
    return t


def _layer(x, mod, states, lw, *, pos_base, mix_T, mix_ns, ffn_T, ffn_ns, pipelined):
    (pre_mix_w, post_mix_w, pre_ffn_w, post_ffn_w, win_b, conv_w, conv_b, dtb, alog, dskip_e,
     normw, wout_b, wup_b, wdn_b, inv2) = lw
    B = x.shape[0]
    sh1, sc1, g1, sh2, sc2, g2 = [m.reshape(B, 1, D_MODEL) for m in jnp.split(mod, 6, axis=-1)]
    mix, r_new, s_new, c_new = _mix_call(
        x, sh1, sc1, pre_mix_w, win_b, inv2, conv_w, conv_b, dtb, alog, dskip_e, normw, states,
        T=mix_T, ns=mix_ns, pos_base=pos_base, pipelined=pipelined)
    y = _ffn_call(x, mix, g1, sh2, sc2, g2, wout_b, post_mix_w, pre_ffn_w, post_ffn_w, wup_b, wdn_b,
                  T=ffn_T, ns=ffn_ns, fc=512)
    return y, r_new, s_new.reshape(B, SSM_HEADS, SSM_HEADDIM, SSM_STATE), c_new


def kernel(x_prompt, x_sample, state_ret, state_ssm, state_conv, c_prompt, c_sample, w_ada, b_ada,
           pre_mix_w, post_mix_w, pre_ffn_w, post_ffn_w, w_in, conv_w, conv_b, dt_bias, a_log, d_skip,
           ssm_norm_w, w_out, w_up, w_down):
    depth = w_ada.shape[0]
    bp, lp, _ = x_prompt.shape
    bs, ls, _ = x_sample.shape
    half = RET_DIM // 2
    inv = ROPE_BASE ** (-jnp.arange(half, dtype=F32) / half)
    inv2 = jnp.concatenate([inv, inv]).reshape(1, RET_DIM)

    def pad_lanes(v):
        return jnp.pad(v.astype(F32), (0, LANES - v.shape[0])).reshape(1, LANES)

    yp, ys = x_prompt, x_sample
    outs = [[] for _ in range(6)]
    rows = bp + bs
    rows_pad = -(-rows // 8) * 8
    for l in range(depth):
        c_all = jnp.pad(jnp.concatenate([c_prompt, c_sample], axis=0), ((0, rows_pad - rows), (0, 0)))
        mod = _modulation(c_all, w_ada[l], b_ada[l])
        win_b = jnp.pad(w_in[l], ((0, 0), (0, IN_COLS_PAD - IN_COLS))).astype(BF16)
        lw = (pre_mix_w[l].reshape(1, -1), post_mix_w[l].reshape(1, -1), pre_ffn_w[l].reshape(1, -1),
              post_ffn_w[l].reshape(1, -1), win_b, conv_w[l], conv_b[l].reshape(1, -1),
              pad_lanes(dt_bias[l]), pad_lanes(a_log[l]),
              jnp.repeat(d_skip[l].astype(F32), SSM_HEADDIM).reshape(1, -1),
              ssm_norm_w[l].reshape(1, -1), w_out[l].astype(BF16), w_up[l].astype(BF16),
              w_down[l].astype(BF16), inv2)
        yp, r, s, c = _layer(yp, mod[:bp], None, lw, pos_base=0,
                             mix_T=_pick(lp, 256), mix_ns=1, ffn_T=_pick(lp, 512), ffn_ns=1,
                             pipelined=False)
        outs[0].append(r); outs[1].append(s); outs[2].append(c)
        ns_mix = _pick(bs, max(1, 128 // ls))
        ns_ffn = _pick(bs, max(1, 512 // ls))
        st = (state_ret[l], state_ssm[l].reshape(bs, SSM_WIDTH, SSM_STATE), state_conv[l])
        ys, r, s, c = _layer(ys, mod[bp:bp + bs], st, lw, pos_base=PAST_LEN,
                             mix_T=ls, mix_ns=ns_mix, ffn_T=ls, ffn_ns=ns_ffn, pipelined=False)
        outs[3].append(r); outs[4].append(s); outs[5].append(c)
    return (yp, ys) + tuple(jnp.stack(o) for o in outs)
```

```python
import functools
import math

import jax
import jax.numpy as jnp
from jax import lax
from jax.experimental import pallas as pl
from jax.experimental.pallas import tpu as pltpu

F32 = jnp.float32
BF16 = jnp.bfloat16

D_MODEL = 2048
PAST_LEN = 4096
CHUNK = 64
RET_HEADS = 8
RET_DIM = 128
RET_WIDTH = RET_HEADS * RET_DIM
SSM_HEADS = 16
SSM_HEADDIM = 64
SSM_WIDTH = SSM_HEADS * SSM_HEADDIM
SSM_GROUPS = 2
HEADS_PER_GROUP = SSM_HEADS // SSM_GROUPS
SSM_STATE = 128
CONV_WIDTH = 4
CONV_DIM = SSM_WIDTH + 2 * SSM_GROUPS * SSM_STATE
D_FF = 4 * D_MODEL
ROPE_BASE = 10000.0
EPS = 1e-6

LANES = 128
CONV_PAD = 8
OFF_Q = 0
OFF_K = RET_WIDTH
OFF_V = 2 * RET_WIDTH
OFF_G = 3 * RET_WIDTH
OFF_Z = 4 * RET_WIDTH
OFF_XBC = 4 * RET_WIDTH + SSM_WIDTH
OFF_DT = OFF_XBC + CONV_DIM
IN_COLS = OFF_DT + SSM_HEADS
IN_COLS_PAD = OFF_DT + LANES
PROJ_BLOCK = 512
FFN_CHUNK = 1024
FFN_ROWS = 16

VMEM_LIMIT = 60 * 1024 * 1024


def _silu(x):
    h = 0.5 * x
    return h + h * jnp.tanh(h)


def _softplus(x):
    return jnp.maximum(x, 0.0) + jnp.log1p(jnp.exp(-jnp.abs(x)))


def _split3(x):
    hi = x.astype(BF16)
    r1 = x - hi.astype(F32)
    mid = r1.astype(BF16)
    lo = (r1 - mid.astype(F32)).astype(BF16)
    return hi, mid, lo


def _mod_kernel(c_ref, w_ref, b_ref, o_ref):
    s = _silu(c_ref[...]).astype(BF16)
    o_ref[...] = jnp.dot(s, w_ref[...].astype(BF16), preferred_element_type=F32) + b_ref[...]


def _modulation(c, w_ada, b_ada):
    rows, d = c.shape
    n = w_ada.shape[1]
    bn = 1024
    return pl.pallas_call(
        _mod_kernel,
        grid=(n // bn,),
        in_specs=[
            pl.BlockSpec((rows, d), lambda j: (0, 0)),
            pl.BlockSpec((d, bn), lambda j: (0, j)),
            pl.BlockSpec((1, bn), lambda j: (0, j)),
        ],
        out_specs=pl.BlockSpec((rows, bn), lambda j: (0, j)),
        out_shape=jax.ShapeDtypeStruct((rows, n), F32),
        compiler_params=pltpu.CompilerParams(
            dimension_semantics=("arbitrary",), vmem_limit_bytes=VMEM_LIMIT),
        name="adaln_mod",
    )(c, w_ada, b_ada.reshape(1, n))


def _mix_kernel(*refs, T, ns, cl, pos_base, has_state, pipelined):
    it = iter(refs)
    x_ref, sh_ref, sc_ref, pw_ref, win_ref, inv_ref = (next(it) for _ in range(6))
    cw_ref, cb_ref, dtb_ref, alog_ref, dsk_ref, nw_ref = (next(it) for _ in range(6))
    if has_state:
        sret_ref, sssm_ref, sconv_ref = (next(it) for _ in range(3))
    mix_ref, ret_ref, ssm_ref, conv_ref = (next(it) for _ in range(4))
    (hb_s, q_s, k_s, kd_s, v_s, gz_s, ext_s, act_s, dt_s, ht_s,
     mask_s, qdec_s, kdec_s) = (next(it) for _ in range(13))

    b_idx = pl.program_id(0)
    t_idx = pl.program_id(1)
    R = ns * T
    lg = [math.log(1.0 - 2.0 ** (-5.0 - h)) for h in range(RET_HEADS)]
    if pipelined:
        sa = t_idx & 1
        sb = 1 - sa
    else:
        sa = sb = 0

    @pl.when((b_idx == 0) & (t_idx == 0))
    def _tables():
        ii = lax.broadcasted_iota(jnp.int32, (T, T), 0)
        jj = lax.broadcasted_iota(jnp.int32, (T, T), 1)
        dist = jnp.abs(ii - jj).astype(F32)
        shift = int(math.log2(cl))
        visible = (jj >> shift) <= (ii >> shift)
        ri = lax.broadcasted_iota(jnp.int32, (T, LANES), 0).astype(F32)
        for h in range(RET_HEADS):
            mask_s[h] = jnp.where(visible, jnp.exp(lg[h] * dist), 0.0)
            qdec_s[h] = jnp.exp(lg[h] * (ri + 1.0))
            kdec_s[h] = jnp.exp(lg[h] * (T - 1.0 - ri))
        if pipelined:
            for buf in (q_s, k_s, kd_s, v_s, gz_s, ext_s, dt_s):
                buf[1] = jnp.zeros(buf.shape[1:], buf.dtype)

    @pl.when(t_idx <= (1 if pipelined else 0))
    def _init():
        if has_state:
            ret_ref[...] = sret_ref[...]
            for s in range(ns):
                for pair in range(SSM_HEADS // 2):
                    pc = slice(pair * LANES, (pair + 1) * LANES)
                    ht_s[s, :, pc] = sssm_ref[s, pc, :].T
        else:
            ret_ref[...] = jnp.zeros(ret_ref.shape, F32)
            ht_s[...] = jnp.zeros(ht_s.shape, F32)
        for slot in range(ext_s.shape[0]):
            ext_s[slot, :, 0:CONV_PAD, :] = jnp.zeros((ns, CONV_PAD, CONV_DIM), F32)
            if has_state:
                ext_s[slot, :, CONV_PAD - (CONV_WIDTH - 1):CONV_PAD, :] = sconv_ref[...]

    pj = {}

    def proj_prologue():
        scale = pw_ref[...] * (1.0 + sc_ref[...])
        rc = min(T, 64)
        for r0 in range(0, T, rc):
            x3 = x_ref[:, r0:r0 + rc, :]
            ms = jnp.mean(x3 * x3, axis=-1, keepdims=True)
            hmod = x3 * lax.rsqrt(ms + EPS) * scale + sh_ref[...]
            for s in range(ns):
                hb_s[s * T + r0:s * T + r0 + rc, :] = hmod[s].astype(BF16)
        hr = R // 2
        row = lax.broadcasted_iota(jnp.int32, (hr, LANES), 0)
        low = lax.broadcasted_iota(jnp.int32, (hr, LANES), 1) < RET_DIM // 2
        row = jnp.where(low, row, row + hr)
        pos = (pos_base + t_idx * T + (row & (T - 1))).astype(F32)
        ang = pos * inv_ref[...]
        c2, s2 = jnp.cos(ang), jnp.sin(ang)
        c2r, s2r = pltpu.roll(c2, RET_DIM // 2, 1), pltpu.roll(s2, RET_DIM // 2, 1)
        pj["cos"] = jnp.concatenate([jnp.where(low, c2, c2r), jnp.where(low, c2r, c2)], axis=0)
        pj["sin"] = jnp.concatenate([jnp.where(low, -s2, s2r), jnp.where(low, -s2r, s2)], axis=0)

    def rot(u):
        return u * pj["cos"] + pltpu.roll(u, RET_DIM // 2, 1) * pj["sin"]

    def put(dst, c0, val):
        w = val.shape[1]
        for s in range(ns):
            dst[sa, s, :, c0:c0 + w] = val[s * T:(s + 1) * T]

    def proj_block(c0):
        res = jnp.dot(hb_s[...], win_ref[:, c0:c0 + PROJ_BLOCK], preferred_element_type=F32)
        if c0 < OFF_V:
            for j in range(PROJ_BLOCK // RET_DIM):
                cj = c0 + j * RET_DIM
                r = rot(res[:, j * RET_DIM:(j + 1) * RET_DIM])
                if c0 < OFF_K:
                    put(q_s, cj, r.astype(BF16))
                else:
                    kf = r * (RET_DIM ** -0.5)
                    h = (cj - OFF_K) // RET_DIM
                    put(k_s, cj - OFF_K, kf.astype(BF16))
                    for s in range(ns):
                        kd_s[sa, s, :, cj - OFF_K:cj - OFF_K + RET_DIM] = (
                            kf[s * T:(s + 1) * T] * kdec_s[h]).astype(BF16)
        elif c0 < OFF_G:
            put(v_s, c0 - OFF_V, res.astype(BF16))
        elif c0 < OFF_XBC:
            put(gz_s, c0 - OFF_G, _silu(res))
        else:
            for s in range(ns):
                ext_s[sa, s, CONV_PAD:CONV_PAD + T, c0 - OFF_XBC:c0 - OFF_XBC + PROJ_BLOCK] = (
                    res[s * T:(s + 1) * T])

    def proj_dt():
        res = jnp.dot(hb_s[...], win_ref[:, OFF_DT:OFF_DT + LANES], preferred_element_type=F32)
        put(dt_s, 0, _softplus(res + dtb_ref[...]))

    proj_units = [proj_prologue]
    proj_units += [functools.partial(proj_block, c0) for c0 in range(0, OFF_DT, PROJ_BLOCK)]
    proj_units += [proj_dt]

    def mixer_units(s):
        mx = {}
        units = []

        def consts():
            ti = lax.broadcasted_iota(jnp.int32, (T, T), 0)
            tj = lax.broadcasted_iota(jnp.int32, (T, T), 1)
            mx["causal"] = tj <= ti
            mx["tri"] = mx["causal"].astype(BF16)
            mx["low"] = lax.broadcasted_iota(jnp.int32, (T, LANES), 1) < SSM_HEADDIM
        units.append(consts)

        def ret_head(h):
            cs = slice(h * RET_DIM, (h + 1) * RET_DIM)
            qh = q_s[sb, s, :, cs]
            kh = k_s[sb, s, :, cs]
            kdh = kd_s[sb, s, :, cs]
            vh = v_s[sb, s, :, cs]
            S_old = ret_ref[s, h]
            sc = lax.dot_general(qh, kh, (((1,), (1,)), ((), ())), preferred_element_type=F32)
            p = (sc * mask_s[h]).astype(BF16)
            o = jnp.dot(p, vh, preferred_element_type=F32)
            o = o + jnp.dot(qh, S_old.astype(BF16), preferred_element_type=F32) * qdec_s[h]
            upd = lax.dot_general(kdh, vh, (((0,), (0,)), ((), ())), preferred_element_type=F32)
            ret_ref[s, h] = math.exp(lg[h] * T) * S_old + upd
            mu = jnp.mean(o, axis=-1, keepdims=True)
            d = o - mu
            var = jnp.mean(d * d, axis=-1, keepdims=True)
            mix_ref[s, :, cs] = (d * lax.rsqrt(var + EPS) * gz_s[sb, s, :, cs]).astype(BF16)
        units += [functools.partial(ret_head, h) for h in range(RET_HEADS)]

        def conv_block(c0):
            cc = slice(c0, c0 + PROJ_BLOCK)
            base = CONV_PAD - (CONV_WIDTH - 1)
            acc = cb_ref[:, cc] + cw_ref[0:1, cc] * ext_s[sb, s, base:base + T, cc]
            for j in range(1, CONV_WIDTH):
                acc = acc + cw_ref[j:j + 1, cc] * ext_s[sb, s, base + j:base + j + T, cc]
            act_s[:, cc] = _silu(acc)
        units += [functools.partial(conv_block, c0) for c0 in range(0, CONV_DIM, PROJ_BLOCK)]

        def conv_tail():
            conv_ref[s] = ext_s[sb, s, CONV_PAD + T - (CONV_WIDTH - 1):CONV_PAD + T, :]
            ext_s[sa, s, 0:CONV_PAD, :] = ext_s[sb, s, T:T + CONV_PAD, :]

        def ssd_cum():
            conv_tail()
            dt = dt_s[sb, s]
            a = dt * (-jnp.exp(alog_ref[...]))
            a_hi, a_mid, a_lo = _split3(a)
            tri = mx["tri"]
            cum = (jnp.dot(tri, a_hi, preferred_element_type=F32)
                   + jnp.dot(tri, a_mid, preferred_element_type=F32)
                   + jnp.dot(tri, a_lo, preferred_element_type=F32))
            mx["cum"] = cum
            mx["cum_t"] = cum.T
            mx["dt_t"] = dt.T
        units.append(ssd_cum)

        gcols = HEADS_PER_GROUP * SSM_HEADDIM

        def ssd_group(g):
            b_f = act_s[:, SSM_WIDTH + g * SSM_STATE:SSM_WIDTH + (g + 1) * SSM_STATE]
            b_g = b_f.astype(BF16)
            c_g = act_s[:, SSM_WIDTH + (SSM_GROUPS + g) * SSM_STATE:
                        SSM_WIDTH + (SSM_GROUPS + g + 1) * SSM_STATE].astype(BF16)
            mx["b_t"] = b_f.T
            mx["gmat"] = lax.dot_general(c_g, b_g, (((1,), (1,)), ((), ())),
                                         preferred_element_type=F32)
            h_old = ht_s[s, :, g * gcols:(g + 1) * gcols]
            mx["y_int"] = jnp.dot(c_g, h_old.astype(BF16), preferred_element_type=F32)

        def ssd_pair(g, m):
            causal, low = mx["causal"], mx["low"]
            cum, cum_t, dt_t = mx["cum"], mx["cum_t"], mx["dt_t"]
            pair = g * (HEADS_PER_GROUP // 2) + m
            h0 = 2 * pair
            pc = slice(pair * LANES, (pair + 1) * LANES)
            ws, bcs, es, decs = [], [], [], []
            for hh in (h0, h0 + 1):
                cbh = jnp.broadcast_to(cum[:, hh:hh + 1], (T, LANES))
                if T >= LANES:
                    colb = jnp.concatenate([cbh] * (T // LANES), axis=1)
                else:
                    colb = cbh[:, :T]
                decay = (jnp.where(causal, jnp.exp(colb - cum_t[hh:hh + 1, :]), 0.0)
                         * dt_t[hh:hh + 1, :])
                ws.append((decay * mx["gmat"]).astype(BF16))
                bcs.append((mx["b_t"] * decay[T - 1:T, :]).astype(BF16))
                es.append(jnp.exp(cbh))
                decs.append(jnp.exp(cum_t[hh:hh + 1, T - 1:T]))
            xs_p = act_s[:, pc]
            xs_b = xs_p.astype(BF16)
            zero = jnp.zeros_like(xs_b)
            rhs = jnp.concatenate([jnp.where(low, xs_b, zero),
                                   jnp.where(low, zero, xs_b)], axis=0)
            y = jnp.dot(jnp.concatenate(ws, axis=1), rhs, preferred_element_type=F32)
            y = (y + mx["y_int"][:, m * LANES:(m + 1) * LANES] * jnp.where(low, es[0], es[1])
                 + dsk_ref[:, pc] * xs_p)
            act_s[:, pc] = y * gz_s[sb, s, :, RET_WIDTH + pair * LANES:RET_WIDTH + (pair + 1) * LANES]
            upd = jnp.dot(jnp.concatenate(bcs, axis=1), rhs, preferred_element_type=F32)
            low_n = lax.broadcasted_iota(jnp.int32, (SSM_STATE, LANES), 1) < SSM_HEADDIM
            ht_s[s, :, pc] = jnp.where(low_n, decs[0], decs[1]) * ht_s[s, :, pc] + upd

        def ssd_state(g):
            for m in range(HEADS_PER_GROUP // 2):
                pair = g * (HEADS_PER_GROUP // 2) + m
                pc = slice(pair * LANES, (pair + 1) * LANES)
                ssm_ref[s, pc, :] = ht_s[s, :, pc].T

        for g in range(SSM_GROUPS):
            units.append(functools.partial(ssd_group, g))
            units += [functools.partial(ssd_pair, g, m) for m in range(HEADS_PER_GROUP // 2)]
            units.append(functools.partial(ssd_state, g))

        def ssd_norm(g):
            gw = SSM_WIDTH // SSM_GROUPS
            gc = slice(g * gw, (g + 1) * gw)
            yz = act_s[:, gc]
            msq = jnp.mean(yz * yz, axis=-1, keepdims=True)
            mix_ref[s, :, RET_WIDTH + g * gw:RET_WIDTH + (g + 1) * gw] = (
                yz * lax.rsqrt(msq + EPS) * nw_ref[:, gc]).astype(BF16)
        units += [functools.partial(ssd_norm, g) for g in range(SSM_GROUPS)]
        return units

    if pipelined:
        a_units, b_units = proj_units, mixer_units(0)
        plan = [[0, 1, 2], [9], [10], [11], [12, 13], [14], [15], [16], [17, 18], [19, 20], [21],
                [22], [23, 24], [25, 26, 3], [4, 5, 6, 7, 8]]
        assert len(plan) == len(a_units) and sorted(sum(plan, [])) == list(range(len(b_units)))
        for au, bis in zip(a_units, plan):
            au()
            for bi in bis:
                b_units[bi]()
    else:
        for au in proj_units:
            au()

        def seq_body(s, carry):
            for bu in mixer_units(s):
                bu()
            return carry

        if ns == 1:
            seq_body(0, 0)
        else:
            lax.fori_loop(0, ns, seq_body, 0)


def _mix_call(x, sh1, sc1, pre_w, win_b, inv2, conv_w, conv_b, dtb, alog, dskip_e, normw,
              states, *, T, ns, pos_base, pipelined):
    B, L, D = x.shape
    has_state = states is not None
    cl = min(L, CHUNK)
    n_t = L // T
    lag = 1 if pipelined else 0
    assert not pipelined or ns == 1
    grid = (B // ns, n_t + lag)
    R = ns * T
    nslot = 2 if pipelined else 1

    def full(shape):
        nd = len(shape)
        return pl.BlockSpec(shape, lambda b, t, _nd=nd: (0,) * _nd, pipeline_mode=pl.Buffered(1))

    in_specs = [
        pl.BlockSpec((ns, T, D), lambda b, t: (b, jnp.minimum(t, n_t - 1), 0)),
        pl.BlockSpec((ns, 1, D), lambda b, t: (b, 0, 0)),
        pl.BlockSpec((ns, 1, D), lambda b, t: (b, 0, 0)),
        full((1, D)),
        full(win_b.shape),
        full((1, LANES)),
        full(conv_w.shape),
        full((1, CONV_DIM)),
        full((1, LANES)),
        full((1, LANES)),
        full((1, SSM_WIDTH)),
        full((1, SSM_WIDTH)),
    ]
    args = [x, sh1, sc1, pre_w, win_b, inv2, conv_w, conv_b, dtb, alog, dskip_e, normw]
    if has_state:
        in_specs += [
            pl.BlockSpec((ns, RET_HEADS, RET_DIM, RET_DIM), lambda b, t: (b, 0, 0, 0)),
            pl.BlockSpec((ns, SSM_WIDTH, SSM_STATE), lambda b, t: (b, 0, 0)),
            pl.BlockSpec((ns, CONV_WIDTH - 1, CONV_DIM), lambda b, t: (b, 0, 0)),
        ]
        args += list(states)
    out_specs = [
        pl.BlockSpec((ns, T, D), lambda b, t: (b, jnp.maximum(t - lag, 0), 0)),
        pl.BlockSpec((ns, RET_HEADS, RET_DIM, RET_DIM), lambda b, t: (b, 0, 0, 0)),
        pl.BlockSpec((ns, SSM_WIDTH, SSM_STATE), lambda b, t: (b, 0, 0)),
        pl.BlockSpec((ns, CONV_WIDTH - 1, CONV_DIM), lambda b, t: (b, 0, 0)),
    ]
    out_shape = [
        jax.ShapeDtypeStruct((B, L, D), BF16),
        jax.ShapeDtypeStruct((B, RET_HEADS, RET_DIM, RET_DIM), F32),
        jax.ShapeDtypeStruct((B, SSM_WIDTH, SSM_STATE), F32),
        jax.ShapeDtypeStruct((B, CONV_WIDTH - 1, CONV_DIM), F32),
    ]
    scratch = [
        pltpu.VMEM((R, D), BF16),
        pltpu.VMEM((nslot, ns, T, RET_WIDTH), BF16),
        pltpu.VMEM((nslot, ns, T, RET_WIDTH), BF16),
        pltpu.VMEM((nslot, ns, T, RET_WIDTH), BF16),
        pltpu.VMEM((nslot, ns, T, RET_WIDTH), BF16),
        pltpu.VMEM((nslot, ns, T, 2 * RET_WIDTH), F32),
        pltpu.VMEM((nslot, ns, T + CONV_PAD, CONV_DIM), F32),
        pltpu.VMEM((T, CONV_DIM), F32),
        pltpu.VMEM((nslot, ns, T, LANES), F32),
        pltpu.VMEM((ns, SSM_STATE, SSM_WIDTH), F32),
        pltpu.VMEM((RET_HEADS, T, T), F32),
        pltpu.VMEM((RET_HEADS, T, LANES), F32),
        pltpu.VMEM((RET_HEADS, T, LANES), F32),
    ]
    kern = functools.partial(_mix_kernel, T=T, ns=ns, cl=cl, pos_base=pos_base, has_state=has_state,
                             pipelined=pipelined)
    return pl.pallas_call(
        kern,
        grid=grid,
        in_specs=in_specs,
        out_specs=out_specs,
        out_shape=out_shape,
        scratch_shapes=scratch,
        compiler_params=pltpu.CompilerParams(
            dimension_semantics=("arbitrary", "arbitrary"), vmem_limit_bytes=VMEM_LIMIT),
        name="token_mix_state" if has_state else "token_mix",
    )(*args)


def _ffn_kernel(x_ref, mix_ref, g1_ref, sh_ref, sc_ref, g2_ref, wout_ref, pmw_ref, pfw_ref, qfw_ref,
                wup_ref, wdn_ref, o_ref, h2_s, acc_s, *, ns, T):
    f = pl.program_id(1)
    R = ns * T
    rc = min(T, FFN_ROWS)

    def inv_rms(v):
        return lax.rsqrt(jnp.mean(v * v, axis=-1, keepdims=True) + EPS)

    @pl.when(f == 0)
    def _first():
        acc_s[...] = jnp.dot(mix_ref[...].reshape(R, D_MODEL), wout_ref[...],
                             preferred_element_type=F32)
        for s in range(ns):
            w1 = pmw_ref[...] * g1_ref[s]
            w2 = pfw_ref[...] * (1.0 + sc_ref[s])
            for r0 in range(0, T, rc):
                rows = slice(s * T + r0, s * T + r0 + rc)
                m = acc_s[rows, :]
                x1 = x_ref[s, r0:r0 + rc, :] + m * inv_rms(m) * w1
                o_ref[s, r0:r0 + rc, :] = x1
                h2_s[rows, :] = (x1 * inv_rms(x1) * w2 + sh_ref[s]).astype(BF16)
                acc_s[rows, :] = jnp.zeros((rc, D_MODEL), F32)

    u = jnp.dot(h2_s[...], wup_ref[...], preferred_element_type=F32)
    u = jnp.maximum(u, 0.0)
    u = (u * u).astype(BF16)
    acc_s[...] += jnp.dot(u, wdn_ref[...], preferred_element_type=F32)

    @pl.when(f == pl.num_programs(1) - 1)
    def _last():
        for s in range(ns):
            w3 = qfw_ref[...] * g2_ref[s]
            for r0 in range(0, T, rc):
                a = acc_s[s * T + r0:s * T + r0 + rc, :]
                o_ref[s, r0:r0 + rc, :] = o_ref[s, r0:r0 + rc, :] + a * inv_rms(a) * w3


def _ffn_call(x, mix, g1, sh2, sc2, g2, wout_b, post_mix_w, pre_ffn_w, post_ffn_w, wup_b, wdn_b,
              *, T, ns, fc):
    B, L, D = x.shape
    grid = (B // ns * (L // T), D_FF // fc)
    nt = L // T

    def tok(i, f):
        return (i // nt, i % nt, 0)

    def seq(i, f):
        return (i // nt, 0, 0)

    def full(shape):
        return pl.BlockSpec(shape, lambda i, f: (0, 0), pipeline_mode=pl.Buffered(1))

    kern = functools.partial(_ffn_kernel, ns=ns, T=T)
    return pl.pallas_call(
        kern,
        grid=grid,
        in_specs=[
            pl.BlockSpec((ns, T, D), tok),
            pl.BlockSpec((ns, T, D), tok),
            pl.BlockSpec((ns, 1, D), seq),
            pl.BlockSpec((ns, 1, D), seq),
            pl.BlockSpec((ns, 1, D), seq),
            pl.BlockSpec((ns, 1, D), seq),
            full((D, D)),
            full((1, D)),
            full((1, D)),
            full((1, D)),
            pl.BlockSpec((D, fc), lambda i, f: (0, f)),
            pl.BlockSpec((fc, D), lambda i, f: (f, 0)),
        ],
        out_specs=pl.BlockSpec((ns, T, D), tok),
        out_shape=jax.ShapeDtypeStruct((B, L, D), F32),
        scratch_shapes=[
            pltpu.VMEM((ns * T, D), BF16),
            pltpu.VMEM((ns * T, D), F32),
        ],
        compiler_params=pltpu.CompilerParams(
            dimension_semantics=("arbitrary", "arbitrary"), vmem_limit_bytes=VMEM_LIMIT),
        name="channel_mix",
    )(x, mix, g1, sh2, sc2, g2, wout_b, post_mix_w, pre_ffn_w, post_ffn_w, wup_b, wdn_b)


def _pick(n, pref):
    t = min(n, pref)
    while n % t:
        t //= 2
    return t


def _layer(x, mod, states, lw, *, pos_base, mix_T, mix_ns, ffn_T, ffn_ns, pipelined):
    (pre_mix_w, post_mix_w, pre_ffn_w, post_ffn_w, win_b, conv_w, conv_b, dtb, alog, dskip_e,
     normw, wout_b, wup_b, wdn_b, inv2) = lw
    B = x.shape[0]
    sh1, sc1, g1, sh2, sc2, g2 = [m.reshape(B, 1, D_MODEL) for m in jnp.split(mod, 6, axis=-1)]
    mix, r_new, s_new, c_new = _mix_call(
        x, sh1, sc1, pre_mix_w, win_b, inv2, conv_w, conv_b, dtb, alog, dskip_e, normw, states,
        T=mix_T, ns=mix_ns, pos_base=pos_base, pipelined=pipelined)
    y = _ffn_call(x, mix, g1, sh2, sc2, g2, wout_b, post_mix_w, pre_ffn_w, post_ffn_w, wup_b, wdn_b,
                  T=ffn_T, ns=ffn_ns, fc=FFN_CHUNK)
    return y, r_new, s_new.reshape(B, SSM_HEADS, SSM_HEADDIM, SSM_STATE), c_new


def kernel(x_prompt, x_sample, state_ret, state_ssm, state_conv, c_prompt, c_sample, w_ada, b_ada,
           pre_mix_w, post_mix_w, pre_ffn_w, post_ffn_w, w_in, conv_w, conv_b, dt_bias, a_log, d_skip,
           ssm_norm_w, w_out, w_up, w_down):
    depth = w_ada.shape[0]
    bp, lp, _ = x_prompt.shape
    bs, ls, _ = x_sample.shape
    half = RET_DIM // 2
    inv = ROPE_BASE ** (-jnp.arange(half, dtype=F32) / half)
    inv2 = jnp.concatenate([inv, inv]).reshape(1, RET_DIM)

    def pad_lanes(v):
        return jnp.pad(v.astype(F32), (0, LANES - v.shape[0])).reshape(1, LANES)

    yp, ys = x_prompt, x_sample
    outs = [[] for _ in range(6)]
    rows = bp + bs
    rows_pad = -(-rows // 8) * 8
    for l in range(depth):
        c_all = jnp.pad(jnp.concatenate([c_prompt, c_sample], axis=0), ((0, rows_pad - rows), (0, 0)))
        mod = _modulation(c_all, w_ada[l], b_ada[l])
        win_b = jnp.pad(w_in[l], ((0, 0), (0, IN_COLS_PAD - IN_COLS))).astype(BF16)
        lw = (pre_mix_w[l].reshape(1, -1), post_mix_w[l].reshape(1, -1), pre_ffn_w[l].reshape(1, -1),
              post_ffn_w[l].reshape(1, -1), win_b, conv_w[l], conv_b[l].reshape(1, -1),
              pad_lanes(dt_bias[l]), pad_lanes(a_log[l]),
              jnp.repeat(d_skip[l].astype(F32), SSM_HEADDIM).reshape(1, -1),
              ssm_norm_w[l].reshape(1, -1), w_out[l].astype(BF16), w_up[l].astype(BF16),
              w_down[l].astype(BF16), inv2)
        yp, r, s, c = _layer(yp, mod[:bp], None, lw, pos_base=0,
                             mix_T=_pick(lp, 256), mix_ns=1, ffn_T=_pick(lp, 512), ffn_ns=1,
                             pipelined=False)
        outs[0].append(r); outs[1].append(s); outs[2].append(c)
        ns_mix = _pick(bs, max(1, 128 // ls))
        ns_ffn = _pick(bs, max(1, 512 // ls))
        st = (state_ret[l], state_ssm[l].reshape(bs, SSM_WIDTH, SSM_STATE), state_conv[l])
        ys, r, s, c = _layer(ys, mod[bp:bp + bs], st, lw, pos_base=PAST_LEN,
                             mix_T=ls, mix_ns=ns_mix, ffn_T=ls, ffn_ns=ns_ffn, pipelined=False)
        outs[3].append(r); outs[4].append(s); outs[5].append(c)
    return (yp, ys) + tuple(jnp.stack(o) for o in outs)
```

```python
import functools
import math

import jax
import jax.numpy as jnp
from jax import lax
from jax.experimental import pallas as pl
from jax.experimental.pallas import tpu as pltpu

F32 = jnp.float32
BF16 = jnp.bfloat16

D_MODEL = 2048
PAST_LEN = 4096
CHUNK = 64
RET_HEADS = 8
RET_DIM = 128
RET_WIDTH = RET_HEADS * RET_DIM
SSM_HEADS = 16
SSM_HEADDIM = 64
SSM_WIDTH = SSM_HEADS * SSM_HEADDIM
SSM_GROUPS = 2
HEADS_PER_GROUP = SSM_HEADS // SSM_GROUPS
SSM_STATE = 128
CONV_WIDTH = 4
CONV_DIM = SSM_WIDTH + 2 * SSM_GROUPS * SSM_STATE
D_FF = 4 * D_MODEL
ROPE_BASE = 10000.0
EPS = 1e-6

LANES = 128
CONV_PAD = 8
OFF_Q = 0
OFF_K = RET_WIDTH
OFF_V = 2 * RET_WIDTH
OFF_G = 3 * RET_WIDTH
OFF_Z = 4 * RET_WIDTH
OFF_XBC = 4 * RET_WIDTH + SSM_WIDTH
OFF_DT = OFF_XBC + CONV_DIM
IN_COLS = OFF_DT + SSM_HEADS
IN_COLS_PAD = OFF_DT + LANES
PROJ_BLOCK = 512
FFN_CHUNK = 1024
FFN_ROWS = 16

VMEM_LIMIT = 60 * 1024 * 1024


def _silu(x):
    h = 0.5 * x
    return h + h * jnp.tanh(h)


def _softplus(x):
    return jnp.maximum(x, 0.0) + jnp.log1p(jnp.exp(-jnp.abs(x)))


def _split3(x):
    hi = x.astype(BF16)
    r1 = x - hi.astype(F32)
    mid = r1.astype(BF16)
    lo = (r1 - mid.astype(F32)).astype(BF16)
    return hi, mid, lo


def _mod_kernel(c_ref, w_ref, b_ref, o_ref):
    s = _silu(c_ref[...]).astype(BF16)
    o_ref[...] = jnp.dot(s, w_ref[...].astype(BF16), preferred_element_type=F32) + b_ref[...]


def _modulation(c, w_ada, b_ada):
    rows, d = c.shape
    n = w_ada.shape[1]
    bn = 1024
    return pl.pallas_call(
        _mod_kernel,
        grid=(n // bn,),
        in_specs=[
            pl.BlockSpec((rows, d), lambda j: (0, 0)),
            pl.BlockSpec((d, bn), lambda j: (0, j)),
            pl.BlockSpec((1, bn), lambda j: (0, j)),
        ],
        out_specs=pl.BlockSpec((rows, bn), lambda j: (0, j)),
        out_shape=jax.ShapeDtypeStruct((rows, n), F32),
        compiler_params=pltpu.CompilerParams(
            dimension_semantics=("arbitrary",), vmem_limit_bytes=VMEM_LIMIT),
        name="adaln_mod",
    )(c, w_ada, b_ada.reshape(1, n))


def _mix_kernel(*refs, T, ns, cl, pos_base, has_state, n_cast):
    it = iter(refs)
    x_ref, sh_ref, sc_ref, pw_ref, win_ref, inv_ref = (next(it) for _ in range(6))
    cw_ref, cb_ref, dtb_ref, alog_ref, dsk_ref, nw_ref = (next(it) for _ in range(6))
    if has_state:
        sret_ref, sssm_ref, sconv_ref = (next(it) for _ in range(3))
    cast_in = [next(it) for _ in range(n_cast)]
    mix_ref, ret_ref, ssm_ref, conv_ref = (next(it) for _ in range(4))
    cast_out = [next(it) for _ in range(n_cast)]
    (hb_s, q_s, k_s, kd_s, v_s, gz_s, ext_s, act_s, dt_s, ht_s,
     mask_s, qdec_s, kdec_s) = (next(it) for _ in range(13))

    b_idx = pl.program_id(0)
    t_idx = pl.program_id(1)
    R = ns * T
    lg = [math.log(1.0 - 2.0 ** (-5.0 - h)) for h in range(RET_HEADS)]

    for wi, wo in zip(cast_in, cast_out):
        wo[...] = wi[...].astype(BF16)

    @pl.when((b_idx == 0) & (t_idx == 0))
    def _tables():
        ii = lax.broadcasted_iota(jnp.int32, (T, T), 0)
        jj = lax.broadcasted_iota(jnp.int32, (T, T), 1)
        dist = jnp.abs(ii - jj).astype(F32)
        shift = int(math.log2(cl))
        visible = (jj >> shift) <= (ii >> shift)
        ri = lax.broadcasted_iota(jnp.int32, (T, LANES), 0).astype(F32)
        for h in range(RET_HEADS):
            mask_s[h] = jnp.where(visible, jnp.exp(lg[h] * dist), 0.0)
            qdec_s[h] = jnp.exp(lg[h] * (ri + 1.0))
            kdec_s[h] = jnp.exp(lg[h] * (T - 1.0 - ri))

    @pl.when(t_idx == 0)
    def _init():
        if has_state:
            ret_ref[...] = sret_ref[...]
            for s in range(ns):
                for pair in range(SSM_HEADS // 2):
                    pc = slice(pair * LANES, (pair + 1) * LANES)
                    ht_s[s, :, pc] = sssm_ref[s, pc, :].T
        else:
            ret_ref[...] = jnp.zeros(ret_ref.shape, F32)
            ht_s[...] = jnp.zeros(ht_s.shape, F32)
        ext_s[:, 0:CONV_PAD, :] = jnp.zeros((ns, CONV_PAD, CONV_DIM), F32)
        if has_state:
            ext_s[:, CONV_PAD - (CONV_WIDTH - 1):CONV_PAD, :] = sconv_ref[...]

    pj = {}

    def proj_prologue():
        scale = pw_ref[...] * (1.0 + sc_ref[...])
        rc = min(T, 64)
        for r0 in range(0, T, rc):
            x3 = x_ref[:, r0:r0 + rc, :]
            ms = jnp.mean(x3 * x3, axis=-1, keepdims=True)
            hmod = x3 * lax.rsqrt(ms + EPS) * scale + sh_ref[...]
            for s in range(ns):
                hb_s[s * T + r0:s * T + r0 + rc, :] = hmod[s].astype(BF16)
        hr = R // 2
        row = lax.broadcasted_iota(jnp.int32, (hr, LANES), 0)
        low = lax.broadcasted_iota(jnp.int32, (hr, LANES), 1) < RET_DIM // 2
        row = jnp.where(low, row, row + hr)
        pos = (pos_base + t_idx * T + (row & (T - 1))).astype(F32)
        ang = pos * inv_ref[...]
        c2, s2 = jnp.cos(ang), jnp.sin(ang)
        c2r, s2r = pltpu.roll(c2, RET_DIM // 2, 1), pltpu.roll(s2, RET_DIM // 2, 1)
        pj["cos"] = jnp.concatenate([jnp.where(low, c2, c2r), jnp.where(low, c2r, c2)], axis=0)
        pj["sin"] = jnp.concatenate([jnp.where(low, -s2, s2r), jnp.where(low, -s2r, s2)], axis=0)

    def rot(u):
        return u * pj["cos"] + pltpu.roll(u, RET_DIM // 2, 1) * pj["sin"]

    def put(dst, c0, val):
        w = val.shape[1]
        for s in range(ns):
            dst[s,:, c0:c0 + w] = val[s * T:(s + 1) * T]

    def proj_block(c0):
        res = jnp.dot(hb_s[...], win_ref[:, c0:c0 + PROJ_BLOCK], preferred_element_type=F32)
        if c0 < OFF_V:
            for j in range(PROJ_BLOCK // RET_DIM):
                cj = c0 + j * RET_DIM
                r = rot(res[:, j * RET_DIM:(j + 1) * RET_DIM])
                if c0 < OFF_K:
                    put(q_s, cj, r.astype(BF16))
                else:
                    kf = r * (RET_DIM ** -0.5)
                    h = (cj - OFF_K) // RET_DIM
                    put(k_s, cj - OFF_K, kf.astype(BF16))
                    for s in range(ns):
                        kd_s[s,:, cj - OFF_K:cj - OFF_K + RET_DIM] = (
                            kf[s * T:(s + 1) * T] * kdec_s[h]).astype(BF16)
        elif c0 < OFF_G:
            put(v_s, c0 - OFF_V, res.astype(BF16))
        elif c0 < OFF_XBC:
            put(gz_s, c0 - OFF_G, _silu(res))
        else:
            for s in range(ns):
                ext_s[s,CONV_PAD:CONV_PAD + T, c0 - OFF_XBC:c0 - OFF_XBC + PROJ_BLOCK] = (
                    res[s * T:(s + 1) * T])

    def proj_dt():
        res = jnp.dot(hb_s[...], win_ref[:, OFF_DT:OFF_DT + LANES], preferred_element_type=F32)
        put(dt_s, 0, _softplus(res + dtb_ref[...]))

    proj_units = [proj_prologue]
    proj_units += [functools.partial(proj_block, c0) for c0 in range(0, OFF_DT, PROJ_BLOCK)]
    proj_units += [proj_dt]

    def mixer_units(s):
        mx = {}
        units = []

        def consts():
            ti = lax.broadcasted_iota(jnp.int32, (T, T), 0)
            tj = lax.broadcasted_iota(jnp.int32, (T, T), 1)
            mx["causal"] = tj <= ti
            mx["tri"] = mx["causal"].astype(BF16)
            mx["low"] = lax.broadcasted_iota(jnp.int32, (T, LANES), 1) < SSM_HEADDIM
        units.append(consts)

        def ret_head(h):
            cs = slice(h * RET_DIM, (h + 1) * RET_DIM)
            qh = q_s[s,:, cs]
            kh = k_s[s,:, cs]
            kdh = kd_s[s,:, cs]
            vh = v_s[s,:, cs]
            S_old = ret_ref[s, h]
            sc = lax.dot_general(qh, kh, (((1,), (1,)), ((), ())), preferred_element_type=F32)
            p = (sc * mask_s[h]).astype(BF16)
            o = jnp.dot(p, vh, preferred_element_type=F32)
            o = o + jnp.dot(qh, S_old.astype(BF16), preferred_element_type=F32) * qdec_s[h]
            upd = lax.dot_general(kdh, vh, (((0,), (0,)), ((), ())), preferred_element_type=F32)
            ret_ref[s, h] = math.exp(lg[h] * T) * S_old + upd
            mu = jnp.mean(o, axis=-1, keepdims=True)
            d = o - mu
            var = jnp.mean(d * d, axis=-1, keepdims=True)
            mix_ref[s, :, cs] = (d * lax.rsqrt(var + EPS) * gz_s[s,:, cs]).astype(BF16)
        units += [functools.partial(ret_head, h) for h in range(RET_HEADS)]

        def conv_block(c0):
            cc = slice(c0, c0 + PROJ_BLOCK)
            base = CONV_PAD - (CONV_WIDTH - 1)
            acc = cb_ref[:, cc] + cw_ref[0:1, cc] * ext_s[s,base:base + T, cc]
            for j in range(1, CONV_WIDTH):
                acc = acc + cw_ref[j:j + 1, cc] * ext_s[s,base + j:base + j + T, cc]
            act_s[:, cc] = _silu(acc)
        units += [functools.partial(conv_block, c0) for c0 in range(0, CONV_DIM, PROJ_BLOCK)]

        def conv_tail():
            conv_ref[s] = ext_s[s,CONV_PAD + T - (CONV_WIDTH - 1):CONV_PAD + T, :]
            ext_s[s,0:CONV_PAD, :] = ext_s[s,T:T + CONV_PAD, :]

        def ssd_cum():
            conv_tail()
            dt = dt_s[s]
            a = dt * (-jnp.exp(alog_ref[...]))
            a_hi, a_mid, a_lo = _split3(a)
            tri = mx["tri"]
            cum = (jnp.dot(tri, a_hi, preferred_element_type=F32)
                   + jnp.dot(tri, a_mid, preferred_element_type=F32)
                   + jnp.dot(tri, a_lo, preferred_element_type=F32))
            mx["cum"] = cum
            mx["cum_t"] = cum.T
            mx["dt_t"] = dt.T
        units.append(ssd_cum)

        gcols = HEADS_PER_GROUP * SSM_HEADDIM

        def ssd_group(g):
            b_f = act_s[:, SSM_WIDTH + g * SSM_STATE:SSM_WIDTH + (g + 1) * SSM_STATE]
            b_g = b_f.astype(BF16)
            c_g = act_s[:, SSM_WIDTH + (SSM_GROUPS + g) * SSM_STATE:
                        SSM_WIDTH + (SSM_GROUPS + g + 1) * SSM_STATE].astype(BF16)
            mx["b_t"] = b_f.T
            mx["gmat"] = lax.dot_general(c_g, b_g, (((1,), (1,)), ((), ())),
                                         preferred_element_type=F32)
            h_old = ht_s[s, :, g * gcols:(g + 1) * gcols]
            mx["y_int"] = jnp.dot(c_g, h_old.astype(BF16), preferred_element_type=F32)

        def ssd_pair(g, m):
            causal, low = mx["causal"], mx["low"]
            cum, cum_t, dt_t = mx["cum"], mx["cum_t"], mx["dt_t"]
            pair = g * (HEADS_PER_GROUP // 2) + m
            h0 = 2 * pair
            pc = slice(pair * LANES, (pair + 1) * LANES)
            ws, bcs, es, decs = [], [], [], []
            for hh in (h0, h0 + 1):
                cbh = jnp.broadcast_to(cum[:, hh:hh + 1], (T, LANES))
                if T >= LANES:
                    colb = jnp.concatenate([cbh] * (T // LANES), axis=1)
                else:
                    colb = cbh[:, :T]
                decay = (jnp.where(causal, jnp.exp(colb - cum_t[hh:hh + 1, :]), 0.0)
                         * dt_t[hh:hh + 1, :])
                ws.append((decay * mx["gmat"]).astype(BF16))
                bcs.append((mx["b_t"] * decay[T - 1:T, :]).astype(BF16))
                es.append(jnp.exp(cbh))
                decs.append(jnp.exp(cum_t[hh:hh + 1, T - 1:T]))
            xs_p = act_s[:, pc]
            xs_b = xs_p.astype(BF16)
            zero = jnp.zeros_like(xs_b)
            rhs = jnp.concatenate([jnp.where(low, xs_b, zero),
                                   jnp.where(low, zero, xs_b)], axis=0)
            y = jnp.dot(jnp.concatenate(ws, axis=1), rhs, preferred_element_type=F32)
            y = (y + mx["y_int"][:, m * LANES:(m + 1) * LANES] * jnp.where(low, es[0], es[1])
                 + dsk_ref[:, pc] * xs_p)
            act_s[:, pc] = y
            upd = jnp.dot(jnp.concatenate(bcs, axis=1), rhs, preferred_element_type=F32)
            low_n = lax.broadcasted_iota(jnp.int32, (SSM_STATE, LANES), 1) < SSM_HEADDIM
            ht_s[s, :, pc] = jnp.where(low_n, decs[0], decs[1]) * ht_s[s, :, pc] + upd

        def ssd_state(g):
            for m in range(HEADS_PER_GROUP // 2):
                pair = g * (HEADS_PER_GROUP // 2) + m
                pc = slice(pair * LANES, (pair + 1) * LANES)
                ssm_ref[s, pc, :] = ht_s[s, :, pc].T

        for g in range(SSM_GROUPS):
            units.append(functools.partial(ssd_group, g))
            units += [functools.partial(ssd_pair, g, m) for m in range(HEADS_PER_GROUP // 2)]
            units.append(functools.partial(ssd_state, g))

        def ssd_norm(g):
            gw = SSM_WIDTH // SSM_GROUPS
            gc = slice(g * gw, (g + 1) * gw)
            yz = act_s[:, gc] * gz_s[s,:, RET_WIDTH + g * gw:RET_WIDTH + (g + 1) * gw]
            msq = jnp.mean(yz * yz, axis=-1, keepdims=True)
            mix_ref[s, :, RET_WIDTH + g * gw:RET_WIDTH + (g + 1) * gw] = (
                yz * lax.rsqrt(msq + EPS) * nw_ref[:, gc]).astype(BF16)
        units += [functools.partial(ssd_norm, g) for g in range(SSM_GROUPS)]
        return units

    if ns == 1:
        blk = {c0: u for c0, u in zip(range(0, OFF_DT, PROJ_BLOCK), proj_units[1:-1])}
        mu = mixer_units(0)
        proj_units[0]()
        mu[0]()
        for c0 in range(OFF_XBC, OFF_DT, PROJ_BLOCK):
            blk[c0]()
        proj_units[-1]()
        order = [c0 for c0 in range(0, OFF_XBC, PROJ_BLOCK)]
        after = [[9], [10], [11], [12, 13], [14, 15], [16, 17, 18], [19, 20, 21], [22, 23, 24],
                 [1, 2, 3, 4], [5, 6, 7, 8]]
        assert len(order) == len(after)
        for c0, bis in zip(order, after):
            blk[c0]()
            for bi in bis:
                mu[bi]()
        mu[25]()
        mu[26]()
    else:
        for au in proj_units:
            au()

        def seq_body(s, carry):
            for bu in mixer_units(s):
                bu()
            return carry

        lax.fori_loop(0, ns, seq_body, 0)


def _mix_call(x, sh1, sc1, pre_w, win_b, inv2, conv_w, conv_b, dtb, alog, dskip_e, normw,
              states, cast_ws, *, T, ns, pos_base):
    B, L, D = x.shape
    has_state = states is not None
    cl = min(L, CHUNK)
    n_t = L // T
    grid = (B // ns, n_t)
    n_steps = grid[0] * grid[1]
    R = ns * T

    def full(shape):
        nd = len(shape)
        return pl.BlockSpec(shape, lambda b, t, _nd=nd: (0,) * _nd, pipeline_mode=pl.Buffered(1))

    in_specs = [
        pl.BlockSpec((ns, T, D), lambda b, t: (b, t, 0)),
        pl.BlockSpec((ns, 1, D), lambda b, t: (b, 0, 0)),
        pl.BlockSpec((ns, 1, D), lambda b, t: (b, 0, 0)),
        full((1, D)),
        full(win_b.shape),
        full((1, LANES)),
        full(conv_w.shape),
        full((1, CONV_DIM)),
        full((1, LANES)),
        full((1, LANES)),
        full((1, SSM_WIDTH)),
        full((1, SSM_WIDTH)),
    ]
    args = [x, sh1, sc1, pre_w, win_b, inv2, conv_w, conv_b, dtb, alog, dskip_e, normw]
    if has_state:
        in_specs += [
            pl.BlockSpec((ns, RET_HEADS, RET_DIM, RET_DIM), lambda b, t: (b, 0, 0, 0)),
            pl.BlockSpec((ns, SSM_WIDTH, SSM_STATE), lambda b, t: (b, 0, 0)),
            pl.BlockSpec((ns, CONV_WIDTH - 1, CONV_DIM), lambda b, t: (b, 0, 0)),
        ]
        args += list(states)
    cast_specs = []
    for w in cast_ws:
        rows = w.shape[0] // n_steps
        assert rows * n_steps == w.shape[0] and rows % 16 == 0
        cast_specs.append(pl.BlockSpec((rows, w.shape[1]), lambda b, t: (b * n_t + t, 0)))
    in_specs += cast_specs
    args += list(cast_ws)
    out_specs = [
        pl.BlockSpec((ns, T, D), lambda b, t: (b, t, 0)),
        pl.BlockSpec((ns, RET_HEADS, RET_DIM, RET_DIM), lambda b, t: (b, 0, 0, 0)),
        pl.BlockSpec((ns, SSM_WIDTH, SSM_STATE), lambda b, t: (b, 0, 0)),
        pl.BlockSpec((ns, CONV_WIDTH - 1, CONV_DIM), lambda b, t: (b, 0, 0)),
    ]
    out_shape = [
        jax.ShapeDtypeStruct((B, L, D), BF16),
        jax.ShapeDtypeStruct((B, RET_HEADS, RET_DIM, RET_DIM), F32),
        jax.ShapeDtypeStruct((B, SSM_WIDTH, SSM_STATE), F32),
        jax.ShapeDtypeStruct((B, CONV_WIDTH - 1, CONV_DIM), F32),
    ]
    out_specs += cast_specs
    out_shape += [jax.ShapeDtypeStruct(w.shape, BF16) for w in cast_ws]
    scratch = [
        pltpu.VMEM((R, D), BF16),
        pltpu.VMEM((ns, T, RET_WIDTH), BF16),
        pltpu.VMEM((ns, T, RET_WIDTH), BF16),
        pltpu.VMEM((ns, T, RET_WIDTH), BF16),
        pltpu.VMEM((ns, T, RET_WIDTH), BF16),
        pltpu.VMEM((ns, T, 2 * RET_WIDTH), F32),
        pltpu.VMEM((ns, T + CONV_PAD, CONV_DIM), F32),
        pltpu.VMEM((T, CONV_DIM), F32),
        pltpu.VMEM((ns, T, LANES), F32),
        pltpu.VMEM((ns, SSM_STATE, SSM_WIDTH), F32),
        pltpu.VMEM((RET_HEADS, T, T), F32),
        pltpu.VMEM((RET_HEADS, T, LANES), F32),
        pltpu.VMEM((RET_HEADS, T, LANES), F32),
    ]
    kern = functools.partial(_mix_kernel, T=T, ns=ns, cl=cl, pos_base=pos_base, has_state=has_state,
                             n_cast=len(cast_ws))
    return pl.pallas_call(
        kern,
        grid=grid,
        in_specs=in_specs,
        out_specs=out_specs,
        out_shape=out_shape,
        scratch_shapes=scratch,
        compiler_params=pltpu.CompilerParams(
            dimension_semantics=("arbitrary", "arbitrary"), vmem_limit_bytes=VMEM_LIMIT),
        name="token_mix_state" if has_state else "token_mix",
    )(*args)


def _ffn_kernel(x_ref, mix_ref, g1_ref, sh_ref, sc_ref, g2_ref, wout_ref, pmw_ref, pfw_ref, qfw_ref,
                wup_ref, wdn_ref, o_ref, h2_s, acc_s, *, ns, T):
    f = pl.program_id(1)
    R = ns * T
    rc = min(T, FFN_ROWS)

    def inv_rms(v):
        return lax.rsqrt(jnp.mean(v * v, axis=-1, keepdims=True) + EPS)

    @pl.when(f == 0)
    def _first():
        acc_s[...] = jnp.dot(mix_ref[...].reshape(R, D_MODEL), wout_ref[...],
                             preferred_element_type=F32)
        for s in range(ns):
            w1 = pmw_ref[...] * g1_ref[s]
            w2 = pfw_ref[...] * (1.0 + sc_ref[s])
            for r0 in range(0, T, rc):
                rows = slice(s * T + r0, s * T + r0 + rc)
                m = acc_s[rows, :]
                x1 = x_ref[s, r0:r0 + rc, :] + m * inv_rms(m) * w1
                o_ref[s, r0:r0 + rc, :] = x1
                h2_s[rows, :] = (x1 * inv_rms(x1) * w2 + sh_ref[s]).astype(BF16)
                acc_s[rows, :] = jnp.zeros((rc, D_MODEL), F32)

    u = jnp.dot(h2_s[...], wup_ref[...], preferred_element_type=F32)
    u = jnp.maximum(u, 0.0)
    u = (u * u).astype(BF16)
    acc_s[...] += jnp.dot(u, wdn_ref[...], preferred_element_type=F32)

    @pl.when(f == pl.num_programs(1) - 1)
    def _last():
        for s in range(ns):
            w3 = qfw_ref[...] * g2_ref[s]
            for r0 in range(0, T, rc):
                a = acc_s[s * T + r0:s * T + r0 + rc, :]
                o_ref[s, r0:r0 + rc, :] = o_ref[s, r0:r0 + rc, :] + a * inv_rms(a) * w3


def _ffn_call(x, mix, g1, sh2, sc2, g2, wout_b, post_mix_w, pre_ffn_w, post_ffn_w, wup_b, wdn_b,
              *, T, ns, fc):
    B, L, D = x.shape
    grid = (B // ns * (L // T), D_FF // fc)
    nt = L // T

    def tok(i, f):
        return (i // nt, i % nt, 0)

    def seq(i, f):
        return (i // nt, 0, 0)

    def full(shape):
        return pl.BlockSpec(shape, lambda i, f: (0, 0), pipeline_mode=pl.Buffered(1))

    kern = functools.partial(_ffn_kernel, ns=ns, T=T)
    return pl.pallas_call(
        kern,
        grid=grid,
        in_specs=[
            pl.BlockSpec((ns, T, D), tok),
            pl.BlockSpec((ns, T, D), tok),
            pl.BlockSpec((ns, 1, D), seq),
            pl.BlockSpec((ns, 1, D), seq),
            pl.BlockSpec((ns, 1, D), seq),
            pl.BlockSpec((ns, 1, D), seq),
            full((D, D)),
            full((1, D)),
            full((1, D)),
            full((1, D)),
            pl.BlockSpec((D, fc), lambda i, f: (0, f)),
            pl.BlockSpec((fc, D), lambda i, f: (f, 0)),
        ],
        out_specs=pl.BlockSpec((ns, T, D), tok),
        out_shape=jax.ShapeDtypeStruct((B, L, D), F32),
        scratch_shapes=[
            pltpu.VMEM((ns * T, D), BF16),
            pltpu.VMEM((ns * T, D), F32),
        ],
        compiler_params=pltpu.CompilerParams(
            dimension_semantics=("arbitrary", "arbitrary"), vmem_limit_bytes=VMEM_LIMIT),
        name="channel_mix",
    )(x, mix, g1, sh2, sc2, g2, wout_b, post_mix_w, pre_ffn_w, post_ffn_w, wup_b, wdn_b)


def _pick(n, pref):
    t = min(n, pref)
    while n % t:
        t //= 2
    return t


def _layer(x, mod, states, lw, ffn_ws, *, pos_base, mix_T, mix_ns, ffn_T, ffn_ns):
    (pre_mix_w, post_mix_w, pre_ffn_w, post_ffn_w, win_b, conv_w, conv_b, dtb, alog, dskip_e,
     normw, inv2) = lw
    B = x.shape[0]
    sh1, sc1, g1, sh2, sc2, g2 = [m.reshape(B, 1, D_MODEL) for m in jnp.split(mod, 6, axis=-1)]
    cast_ws = tuple(w for w in ffn_ws if w.dtype != BF16)
    res = _mix_call(
        x, sh1, sc1, pre_mix_w, win_b, inv2, conv_w, conv_b, dtb, alog, dskip_e, normw, states,
        cast_ws, T=mix_T, ns=mix_ns, pos_base=pos_base)
    mix, r_new, s_new, c_new = res[:4]
    if cast_ws:
        ffn_ws = tuple(res[4:])
    wout_b, wup_b, wdn_b = ffn_ws
    y = _ffn_call(x, mix, g1, sh2, sc2, g2, wout_b, post_mix_w, pre_ffn_w, post_ffn_w, wup_b, wdn_b,
                  T=ffn_T, ns=ffn_ns, fc=FFN_CHUNK)
    return y, r_new, s_new.reshape(B, SSM_HEADS, SSM_HEADDIM, SSM_STATE), c_new, ffn_ws


def kernel(x_prompt, x_sample, state_ret, state_ssm, state_conv, c_prompt, c_sample, w_ada, b_ada,
           pre_mix_w, post_mix_w, pre_ffn_w, post_ffn_w, w_in, conv_w, conv_b, dt_bias, a_log, d_skip,
           ssm_norm_w, w_out, w_up, w_down):
    depth = w_ada.shape[0]
    bp, lp, _ = x_prompt.shape
    bs, ls, _ = x_sample.shape
    half = RET_DIM // 2
    inv = ROPE_BASE ** (-jnp.arange(half, dtype=F32) / half)
    inv2 = jnp.concatenate([inv, inv]).reshape(1, RET_DIM)

    def pad_lanes(v):
        return jnp.pad(v.astype(F32), (0, LANES - v.shape[0])).reshape(1, LANES)

    yp, ys = x_prompt, x_sample
    outs = [[] for _ in range(6)]
    rows = bp + bs
    rows_pad = -(-rows // 8) * 8
    for l in range(depth):
        c_all = jnp.pad(jnp.concatenate([c_prompt, c_sample], axis=0), ((0, rows_pad - rows), (0, 0)))
        mod = _modulation(c_all, w_ada[l], b_ada[l])
        win_b = jnp.pad(w_in[l].astype(BF16), ((0, 0), (0, IN_COLS_PAD - IN_COLS)))
        lw = (pre_mix_w[l].reshape(1, -1), post_mix_w[l].reshape(1, -1), pre_ffn_w[l].reshape(1, -1),
              post_ffn_w[l].reshape(1, -1), win_b, conv_w[l], conv_b[l].reshape(1, -1),
              pad_lanes(dt_bias[l]), pad_lanes(a_log[l]),
              jnp.repeat(d_skip[l].astype(F32), SSM_HEADDIM).reshape(1, -1),
              ssm_norm_w[l].reshape(1, -1), inv2)
        mix_T = _pick(lp, 256)
        ffn_ws = (w_out[l], w_up[l], w_down[l])
        n_steps = bp * (lp // mix_T)
        if any(w.shape[0] % (16 * n_steps) for w in ffn_ws):
            ffn_ws = tuple(w.astype(BF16) for w in ffn_ws)
        yp, r, s, c, ffn_ws = _layer(yp, mod[:bp], None, lw, ffn_ws, pos_base=0,
                                     mix_T=mix_T, mix_ns=1, ffn_T=_pick(lp, 512), ffn_ns=1)
        outs[0].append(r); outs[1].append(s); outs[2].append(c)
        ns_mix = _pick(bs, max(1, 128 // ls))
        ns_ffn = _pick(bs, max(1, 512 // ls))
        st = (state_ret[l], state_ssm[l].reshape(bs, SSM_WIDTH, SSM_STATE), state_conv[l])
        ys, r, s, c, _ = _layer(ys, mod[bp:bp + bs], st, lw, ffn_ws, pos_base=PAST_LEN,
                                mix_T=ls, mix_ns=ns_mix, ffn_T=ls, ffn_ns=ns_ffn)
        outs[3].append(r); outs[4].append(s); outs[5].append(c)
    return (yp, ys) + tuple(jnp.stack(o) for o in outs)
```

```python
import functools
import math

import jax
import jax.numpy as jnp
from jax import lax
from jax.experimental import pallas as pl
from jax.experimental.pallas import tpu as pltpu

F32 = jnp.float32
BF16 = jnp.bfloat16

D_MODEL = 2048
PAST_LEN = 4096
CHUNK = 64
RET_HEADS = 8
RET_DIM = 128
RET_WIDTH = RET_HEADS * RET_DIM
SSM_HEADS = 16
SSM_HEADDIM = 64
SSM_WIDTH = SSM_HEADS * SSM_HEADDIM
SSM_GROUPS = 2
HEADS_PER_GROUP = SSM_HEADS // SSM_GROUPS
SSM_STATE = 128
CONV_WIDTH = 4
CONV_DIM = SSM_WIDTH + 2 * SSM_GROUPS * SSM_STATE
D_FF = 4 * D_MODEL
ROPE_BASE = 10000.0
EPS = 1e-6

LANES = 128
CONV_PAD = 8
OFF_Q = 0
OFF_K = RET_WIDTH
OFF_V = 2 * RET_WIDTH
OFF_G = 3 * RET_WIDTH
OFF_Z = 4 * RET_WIDTH
OFF_XBC = 4 * RET_WIDTH + SSM_WIDTH
OFF_DT = OFF_XBC + CONV_DIM
IN_COLS = OFF_DT + SSM_HEADS
IN_COLS_PAD = OFF_DT + LANES
PROJ_BLOCK = 512
FFN_CHUNK = 1024
FFN_ROWS = 16

VMEM_LIMIT = 60 * 1024 * 1024


def _silu(x):
    h = 0.5 * x
    return h + h * jnp.tanh(h)


def _softplus(x):
    return jnp.maximum(x, 0.0) + jnp.log1p(jnp.exp(-jnp.abs(x)))


def _split3(x):
    hi = x.astype(BF16)
    r1 = x - hi.astype(F32)
    mid = r1.astype(BF16)
    lo = (r1 - mid.astype(F32)).astype(BF16)
    return hi, mid, lo


def _mod_kernel(c_ref, w_ref, b_ref, win_ref, o_ref, winb_ref):
    s = _silu(c_ref[...]).astype(BF16)
    o_ref[...] = jnp.dot(s, w_ref[...].astype(BF16), preferred_element_type=F32) + b_ref[...]
    winb_ref[:, 0:IN_COLS] = win_ref[...].astype(BF16)
    winb_ref[:, IN_COLS:IN_COLS_PAD] = jnp.zeros((winb_ref.shape[0], IN_COLS_PAD - IN_COLS), BF16)


def _modulation(c, w_ada, b_ada, w_in):
    rows, d = c.shape
    n = w_ada.shape[1]
    steps = 16
    bn = n // steps
    wr = w_in.shape[0] // steps
    assert bn * steps == n and bn % LANES == 0 and wr * steps == w_in.shape[0] and wr % 16 == 0
    return pl.pallas_call(
        _mod_kernel,
        grid=(steps,),
        in_specs=[
            pl.BlockSpec((rows, d), lambda j: (0, 0)),
            pl.BlockSpec((d, bn), lambda j: (0, j)),
            pl.BlockSpec((1, bn), lambda j: (0, j)),
            pl.BlockSpec((wr, IN_COLS), lambda j: (j, 0)),
        ],
        out_specs=[pl.BlockSpec((rows, bn), lambda j: (0, j)),
                   pl.BlockSpec((wr, IN_COLS_PAD), lambda j: (j, 0))],
        out_shape=[jax.ShapeDtypeStruct((rows, n), F32),
                   jax.ShapeDtypeStruct((w_in.shape[0], IN_COLS_PAD), BF16)],
        compiler_params=pltpu.CompilerParams(
            dimension_semantics=("arbitrary",), vmem_limit_bytes=VMEM_LIMIT),
        name="adaln_mod",
    )(c, w_ada, b_ada.reshape(1, n), w_in)


def _mix_kernel(*refs, T, ns, cl, pos_base, has_state, n_cast):
    it = iter(refs)
    x_ref, sh_ref, sc_ref, pw_ref, win_ref, inv_ref = (next(it) for _ in range(6))
    cw_ref, cb_ref, dtb_ref, alog_ref, dsk_ref, nw_ref = (next(it) for _ in range(6))
    if has_state:
        sret_ref, sssm_ref, sconv_ref = (next(it) for _ in range(3))
    cast_in = [next(it) for _ in range(n_cast)]
    mix_ref, ret_ref, ssm_ref, conv_ref = (next(it) for _ in range(4))
    cast_out = [next(it) for _ in range(n_cast)]
    (hb_s, q_s, k_s, kd_s, v_s, gz_s, ext_s, act_s, dt_s, ht_s,
     mask_s, qdec_s, kdec_s) = (next(it) for _ in range(13))

    b_idx = pl.program_id(0)
    t_idx = pl.program_id(1)
    R = ns * T
    lg = [math.log(1.0 - 2.0 ** (-5.0 - h)) for h in range(RET_HEADS)]

    for wi, wo in zip(cast_in, cast_out):
        wo[...] = wi[...].astype(BF16)

    @pl.when((b_idx == 0) & (t_idx == 0))
    def _tables():
        ii = lax.broadcasted_iota(jnp.int32, (T, T), 0)
        jj = lax.broadcasted_iota(jnp.int32, (T, T), 1)
        dist = jnp.abs(ii - jj).astype(F32)
        shift = int(math.log2(cl))
        visible = (jj >> shift) <= (ii >> shift)
        ri = lax.broadcasted_iota(jnp.int32, (T, LANES), 0).astype(F32)
        for h in range(RET_HEADS):
            mask_s[h] = jnp.where(visible, jnp.exp(lg[h] * dist), 0.0)
            qdec_s[h] = jnp.exp(lg[h] * (ri + 1.0))
            kdec_s[h] = jnp.exp(lg[h] * (T - 1.0 - ri))

    @pl.when(t_idx == 0)
    def _init():
        if has_state:
            ret_ref[...] = sret_ref[...]
            for s in range(ns):
                for pair in range(SSM_HEADS // 2):
                    pc = slice(pair * LANES, (pair + 1) * LANES)
                    ht_s[s, :, pc] = sssm_ref[s, pc, :].T
        else:
            ret_ref[...] = jnp.zeros(ret_ref.shape, F32)
            ht_s[...] = jnp.zeros(ht_s.shape, F32)
        ext_s[:, 0:CONV_PAD, :] = jnp.zeros((ns, CONV_PAD, CONV_DIM), F32)
        if has_state:
            ext_s[:, CONV_PAD - (CONV_WIDTH - 1):CONV_PAD, :] = sconv_ref[...]

    pj = {}

    def proj_prologue():
        scale = pw_ref[...] * (1.0 + sc_ref[...])
        rc = min(T, 64)
        for r0 in range(0, T, rc):
            x3 = x_ref[:, r0:r0 + rc, :]
            ms = jnp.mean(x3 * x3, axis=-1, keepdims=True)
            hmod = x3 * lax.rsqrt(ms + EPS) * scale + sh_ref[...]
            for s in range(ns):
                hb_s[s * T + r0:s * T + r0 + rc, :] = hmod[s].astype(BF16)
        hr = R // 2
        row = lax.broadcasted_iota(jnp.int32, (hr, LANES), 0)
        low = lax.broadcasted_iota(jnp.int32, (hr, LANES), 1) < RET_DIM // 2
        row = jnp.where(low, row, row + hr)
        pos = (pos_base + t_idx * T + (row & (T - 1))).astype(F32)
        ang = pos * inv_ref[...]
        c2, s2 = jnp.cos(ang), jnp.sin(ang)
        c2r, s2r = pltpu.roll(c2, RET_DIM // 2, 1), pltpu.roll(s2, RET_DIM // 2, 1)
        pj["cos"] = jnp.concatenate([jnp.where(low, c2, c2r), jnp.where(low, c2r, c2)], axis=0)
        pj["sin"] = jnp.concatenate([jnp.where(low, -s2, s2r), jnp.where(low, -s2r, s2)], axis=0)

    def rot(u):
        return u * pj["cos"] + pltpu.roll(u, RET_DIM // 2, 1) * pj["sin"]

    def put(dst, c0, val):
        w = val.shape[1]
        for s in range(ns):
            dst[s,:, c0:c0 + w] = val[s * T:(s + 1) * T]

    def proj_block(c0):
        res = jnp.dot(hb_s[...], win_ref[:, c0:c0 + PROJ_BLOCK], preferred_element_type=F32)
        if c0 < OFF_V:
            for j in range(PROJ_BLOCK // RET_DIM):
                cj = c0 + j * RET_DIM
                r = rot(res[:, j * RET_DIM:(j + 1) * RET_DIM])
                if c0 < OFF_K:
                    put(q_s, cj, r.astype(BF16))
                else:
                    kf = r * (RET_DIM ** -0.5)
                    h = (cj - OFF_K) // RET_DIM
                    put(k_s, cj - OFF_K, kf.astype(BF16))
                    for s in range(ns):
                        kd_s[s,:, cj - OFF_K:cj - OFF_K + RET_DIM] = (
                            kf[s * T:(s + 1) * T] * kdec_s[h]).astype(BF16)
        elif c0 < OFF_G:
            put(v_s, c0 - OFF_V, res.astype(BF16))
        elif c0 < OFF_XBC:
            put(gz_s, c0 - OFF_G, _silu(res))
        else:
            for s in range(ns):
                ext_s[s,CONV_PAD:CONV_PAD + T, c0 - OFF_XBC:c0 - OFF_XBC + PROJ_BLOCK] = (
                    res[s * T:(s + 1) * T])

    def proj_dt():
        res = jnp.dot(hb_s[...], win_ref[:, OFF_DT:OFF_DT + LANES], preferred_element_type=F32)
        put(dt_s, 0, _softplus(res + dtb_ref[...]))

    proj_units = [proj_prologue]
    proj_units += [functools.partial(proj_block, c0) for c0 in range(0, OFF_DT, PROJ_BLOCK)]
    proj_units += [proj_dt]

    def mixer_units(s):
        mx = {}
        units = []

        def consts():
            ti = lax.broadcasted_iota(jnp.int32, (T, T), 0)
            tj = lax.broadcasted_iota(jnp.int32, (T, T), 1)
            mx["causal"] = tj <= ti
            mx["tri"] = mx["causal"].astype(BF16)
            mx["low"] = lax.broadcasted_iota(jnp.int32, (T, LANES), 1) < SSM_HEADDIM
        units.append(consts)

        def ret_head(h):
            cs = slice(h * RET_DIM, (h + 1) * RET_DIM)
            qh = q_s[s,:, cs]
            kh = k_s[s,:, cs]
            kdh = kd_s[s,:, cs]
            vh = v_s[s,:, cs]
            S_old = ret_ref[s, h]
            sc = lax.dot_general(qh, kh, (((1,), (1,)), ((), ())), preferred_element_type=F32)
            p = (sc * mask_s[h]).astype(BF16)
            o = jnp.dot(p, vh, preferred_element_type=F32)
            o = o + jnp.dot(qh, S_old.astype(BF16), preferred_element_type=F32) * qdec_s[h]
            upd = lax.dot_general(kdh, vh, (((0,), (0,)), ((), ())), preferred_element_type=F32)
            ret_ref[s, h] = math.exp(lg[h] * T) * S_old + upd
            mu = jnp.mean(o, axis=-1, keepdims=True)
            d = o - mu
            var = jnp.mean(d * d, axis=-1, keepdims=True)
            mix_ref[s, :, cs] = (d * lax.rsqrt(var + EPS) * gz_s[s,:, cs]).astype(BF16)
        units += [functools.partial(ret_head, h) for h in range(RET_HEADS)]

        def conv_block(c0):
            cc = slice(c0, c0 + PROJ_BLOCK)
            base = CONV_PAD - (CONV_WIDTH - 1)
            acc = cb_ref[:, cc] + cw_ref[0:1, cc] * ext_s[s,base:base + T, cc]
            for j in range(1, CONV_WIDTH):
                acc = acc + cw_ref[j:j + 1, cc] * ext_s[s,base + j:base + j + T, cc]
            act_s[:, cc] = _silu(acc)
        units += [functools.partial(conv_block, c0) for c0 in range(0, CONV_DIM, PROJ_BLOCK)]

        def conv_tail():
            conv_ref[s] = ext_s[s,CONV_PAD + T - (CONV_WIDTH - 1):CONV_PAD + T, :]
            ext_s[s,0:CONV_PAD, :] = ext_s[s,T:T + CONV_PAD, :]

        def ssd_cum():
            conv_tail()
            dt = dt_s[s]
            a = dt * (-jnp.exp(alog_ref[...]))
            a_hi, a_mid, a_lo = _split3(a)
            tri = mx["tri"]
            cum = (jnp.dot(tri, a_hi, preferred_element_type=F32)
                   + jnp.dot(tri, a_mid, preferred_element_type=F32)
                   + jnp.dot(tri, a_lo, preferred_element_type=F32))
            mx["cum"] = cum
            mx["cum_t"] = cum.T
            mx["dt_t"] = dt.T
        units.append(ssd_cum)

        gcols = HEADS_PER_GROUP * SSM_HEADDIM

        def ssd_group(g):
            b_f = act_s[:, SSM_WIDTH + g * SSM_STATE:SSM_WIDTH + (g + 1) * SSM_STATE]
            b_g = b_f.astype(BF16)
            c_g = act_s[:, SSM_WIDTH + (SSM_GROUPS + g) * SSM_STATE:
                        SSM_WIDTH + (SSM_GROUPS + g + 1) * SSM_STATE].astype(BF16)
            mx["b_t"] = b_f.T
            mx["gmat"] = lax.dot_general(c_g, b_g, (((1,), (1,)), ((), ())),
                                         preferred_element_type=F32)
            h_old = ht_s[s, :, g * gcols:(g + 1) * gcols]
            mx["y_int"] = jnp.dot(c_g, h_old.astype(BF16), preferred_element_type=F32)

        def ssd_pair(g, m):
            causal, low = mx["causal"], mx["low"]
            cum, cum_t, dt_t = mx["cum"], mx["cum_t"], mx["dt_t"]
            pair = g * (HEADS_PER_GROUP // 2) + m
            h0 = 2 * pair
            pc = slice(pair * LANES, (pair + 1) * LANES)
            ws, bcs, es, decs = [], [], [], []
            for hh in (h0, h0 + 1):
                cbh = jnp.broadcast_to(cum[:, hh:hh + 1], (T, LANES))
                if T >= LANES:
                    colb = jnp.concatenate([cbh] * (T // LANES), axis=1)
                else:
                    colb = cbh[:, :T]
                decay = (jnp.where(causal, jnp.exp(colb - cum_t[hh:hh + 1, :]), 0.0)
                         * dt_t[hh:hh + 1, :])
                ws.append((decay * mx["gmat"]).astype(BF16))
                bcs.append((mx["b_t"] * decay[T - 1:T, :]).astype(BF16))
                es.append(jnp.exp(cbh))
                decs.append(jnp.exp(cum_t[hh:hh + 1, T - 1:T]))
            xs_p = act_s[:, pc]
            xs_b = xs_p.astype(BF16)
            zero = jnp.zeros_like(xs_b)
            rhs = jnp.concatenate([jnp.where(low, xs_b, zero),
                                   jnp.where(low, zero, xs_b)], axis=0)
            y = jnp.dot(jnp.concatenate(ws, axis=1), rhs, preferred_element_type=F32)
            y = (y + mx["y_int"][:, m * LANES:(m + 1) * LANES] * jnp.where(low, es[0], es[1])
                 + dsk_ref[:, pc] * xs_p)
            act_s[:, pc] = y
            upd = jnp.dot(jnp.concatenate(bcs, axis=1), rhs, preferred_element_type=F32)
            low_n = lax.broadcasted_iota(jnp.int32, (SSM_STATE, LANES), 1) < SSM_HEADDIM
            ht_s[s, :, pc] = jnp.where(low_n, decs[0], decs[1]) * ht_s[s, :, pc] + upd

        def ssd_state(g):
            for m in range(HEADS_PER_GROUP // 2):
                pair = g * (HEADS_PER_GROUP // 2) + m
                pc = slice(pair * LANES, (pair + 1) * LANES)
                ssm_ref[s, pc, :] = ht_s[s, :, pc].T

        for g in range(SSM_GROUPS):
            units.append(functools.partial(ssd_group, g))
            units += [functools.partial(ssd_pair, g, m) for m in range(HEADS_PER_GROUP // 2)]
            units.append(functools.partial(ssd_state, g))

        def ssd_norm(g):
            gw = SSM_WIDTH // SSM_GROUPS
            gc = slice(g * gw, (g + 1) * gw)
            yz = act_s[:, gc] * gz_s[s,:, RET_WIDTH + g * gw:RET_WIDTH + (g + 1) * gw]
            msq = jnp.mean(yz * yz, axis=-1, keepdims=True)
            mix_ref[s, :, RET_WIDTH + g * gw:RET_WIDTH + (g + 1) * gw] = (
                yz * lax.rsqrt(msq + EPS) * nw_ref[:, gc]).astype(BF16)
        units += [functools.partial(ssd_norm, g) for g in range(SSM_GROUPS)]
        return units

    if ns == 1:
        blk = {c0: u for c0, u in zip(range(0, OFF_DT, PROJ_BLOCK), proj_units[1:-1])}
        mu = mixer_units(0)
        proj_units[0]()
        mu[0]()
        for c0 in range(OFF_XBC, OFF_DT, PROJ_BLOCK):
            blk[c0]()
        proj_units[-1]()
        order = [c0 for c0 in range(0, OFF_XBC, PROJ_BLOCK)]
        after = [[9], [10], [11], [12, 13], [14, 15], [16, 17, 18], [19, 20, 21], [22, 23, 24],
                 [1, 2, 3, 4], [5, 6, 7, 8]]
        assert len(order) == len(after)
        for c0, bis in zip(order, after):
            blk[c0]()
            for bi in bis:
                mu[bi]()
        mu[25]()
        mu[26]()
    else:
        for au in proj_units:
            au()

        def seq_body(s, carry):
            for bu in mixer_units(s):
                bu()
            return carry

        lax.fori_loop(0, ns, seq_body, 0)


def _mix_call(x, sh1, sc1, pre_w, win_b, inv2, conv_w, conv_b, dtb, alog, dskip_e, normw,
              states, cast_ws, *, T, ns, pos_base):
    B, L, D = x.shape
    has_state = states is not None
    cl = min(L, CHUNK)
    n_t = L // T
    grid = (B // ns, n_t)
    n_steps = grid[0] * grid[1]
    R = ns * T

    def full(shape):
        nd = len(shape)
        return pl.BlockSpec(shape, lambda b, t, _nd=nd: (0,) * _nd, pipeline_mode=pl.Buffered(1))

    in_specs = [
        pl.BlockSpec((ns, T, D), lambda b, t: (b, t, 0)),
        pl.BlockSpec((ns, 1, D), lambda b, t: (b, 0, 0)),
        pl.BlockSpec((ns, 1, D), lambda b, t: (b, 0, 0)),
        full((1, D)),
        full(win_b.shape),
        full((1, LANES)),
        full(conv_w.shape),
        full((1, CONV_DIM)),
        full((1, LANES)),
        full((1, LANES)),
        full((1, SSM_WIDTH)),
        full((1, SSM_WIDTH)),
    ]
    args = [x, sh1, sc1, pre_w, win_b, inv2, conv_w, conv_b, dtb, alog, dskip_e, normw]
    if has_state:
        in_specs += [
            pl.BlockSpec((ns, RET_HEADS, RET_DIM, RET_DIM), lambda b, t: (b, 0, 0, 0)),
            pl.BlockSpec((ns, SSM_WIDTH, SSM_STATE), lambda b, t: (b, 0, 0)),
            pl.BlockSpec((ns, CONV_WIDTH - 1, CONV_DIM), lambda b, t: (b, 0, 0)),
        ]
        args += list(states)
    cast_specs = []
    for w in cast_ws:
        rows = w.shape[0] // n_steps
        assert rows * n_steps == w.shape[0] and rows % 16 == 0
        cast_specs.append(pl.BlockSpec((rows, w.shape[1]), lambda b, t: (b * n_t + t, 0)))
    in_specs += cast_specs
    args += list(cast_ws)
    out_specs = [
        pl.BlockSpec((ns, T, D), lambda b, t: (b, t, 0)),
        pl.BlockSpec((ns, RET_HEADS, RET_DIM, RET_DIM), lambda b, t: (b, 0, 0, 0)),
        pl.BlockSpec((ns, SSM_WIDTH, SSM_STATE), lambda b, t: (b, 0, 0)),
        pl.BlockSpec((ns, CONV_WIDTH - 1, CONV_DIM), lambda b, t: (b, 0, 0)),
    ]
    out_shape = [
        jax.ShapeDtypeStruct((B, L, D), BF16),
        jax.ShapeDtypeStruct((B, RET_HEADS, RET_DIM, RET_DIM), F32),
        jax.ShapeDtypeStruct((B, SSM_WIDTH, SSM_STATE), F32),
        jax.ShapeDtypeStruct((B, CONV_WIDTH - 1, CONV_DIM), F32),
    ]
    out_specs += cast_specs
    out_shape += [jax.ShapeDtypeStruct(w.shape, BF16) for w in cast_ws]
    scratch = [
        pltpu.VMEM((R, D), BF16),
        pltpu.VMEM((ns, T, RET_WIDTH), BF16),
        pltpu.VMEM((ns, T, RET_WIDTH), BF16),
        pltpu.VMEM((ns, T, RET_WIDTH), BF16),
        pltpu.VMEM((ns, T, RET_WIDTH), BF16),
        pltpu.VMEM((ns, T, 2 * RET_WIDTH), F32),
        pltpu.VMEM((ns, T + CONV_PAD, CONV_DIM), F32),
        pltpu.VMEM((T, CONV_DIM), F32),
        pltpu.VMEM((ns, T, LANES), F32),
        pltpu.VMEM((ns, SSM_STATE, SSM_WIDTH), F32),
        pltpu.VMEM((RET_HEADS, T, T), F32),
        pltpu.VMEM((RET_HEADS, T, LANES), F32),
        pltpu.VMEM((RET_HEADS, T, LANES), F32),
    ]
    kern = functools.partial(_mix_kernel, T=T, ns=ns, cl=cl, pos_base=pos_base, has_state=has_state,
                             n_cast=len(cast_ws))
    return pl.pallas_call(
        kern,
        grid=grid,
        in_specs=in_specs,
        out_specs=out_specs,
        out_shape=out_shape,
        scratch_shapes=scratch,
        compiler_params=pltpu.CompilerParams(
            dimension_semantics=("arbitrary", "arbitrary"), vmem_limit_bytes=VMEM_LIMIT),
        name="token_mix_state" if has_state else "token_mix",
    )(*args)


def _ffn_kernel(x_ref, mix_ref, g1_ref, sh_ref, sc_ref, g2_ref, wout_ref, pmw_ref, pfw_ref, qfw_ref,
                wup_ref, wdn_ref, o_ref, h2_s, acc_s, *, ns, T, nf):
    f = pl.program_id(1)
    R = ns * T
    rc = min(T, FFN_ROWS)
    halves = ((0, R // 2), (R // 2, R))

    def inv_rms(v):
        return lax.rsqrt(jnp.mean(v * v, axis=-1, keepdims=True) + EPS)

    def passes(lo, hi):
        return [(s, r0) for s in range(ns) for r0 in range(0, T, rc) if lo <= s * T + r0 < hi]

    def out_proj(lo, hi):
        lhs = mix_ref[0, lo:hi, :] if ns == 1 else mix_ref[lo // T:hi // T].reshape(hi - lo, D_MODEL)
        acc_s[lo:hi, :] = jnp.dot(lhs, wout_ref[...], preferred_element_type=F32)

    def first_norms(lo, hi):
        for s, r0 in passes(lo, hi):
            rows = slice(s * T + r0, s * T + r0 + rc)
            m = acc_s[rows, :]
            x1 = x_ref[s, r0:r0 + rc, :] + m * inv_rms(m) * (pmw_ref[...] * g1_ref[s])
            o_ref[s, r0:r0 + rc, :] = x1
            h2_s[rows, :] = (x1 * inv_rms(x1) * (pfw_ref[...] * (1.0 + sc_ref[s])) + sh_ref[s]).astype(BF16)
            acc_s[rows, :] = jnp.zeros((rc, D_MODEL), F32)

    def mlp(lo, hi):
        u = jnp.dot(h2_s[lo:hi, :], wup_ref[...], preferred_element_type=F32)
        u = jnp.maximum(u, 0.0)
        u = (u * u).astype(BF16)
        acc_s[lo:hi, :] += jnp.dot(u, wdn_ref[...], preferred_element_type=F32)

    def last_norms(lo, hi):
        for s, r0 in passes(lo, hi):
            a = acc_s[s * T + r0:s * T + r0 + rc, :]
            o_ref[s, r0:r0 + rc, :] = (o_ref[s, r0:r0 + rc, :]
                                       + a * inv_rms(a) * (qfw_ref[...] * g2_ref[s]))

    @pl.when(f == 0)
    def _first():
        for lo, hi in halves:
            out_proj(lo, hi)
        for lo, hi in halves:
            first_norms(lo, hi)
            mlp(lo, hi)

    @pl.when((f > 0) & (f < nf - 1))
    def _middle():
        mlp(0, R)

    @pl.when(f == nf - 1)
    def _last():
        for lo, hi in halves:
            mlp(lo, hi)
        for lo, hi in halves:
            last_norms(lo, hi)


def _ffn_call(x, mix, g1, sh2, sc2, g2, wout_b, post_mix_w, pre_ffn_w, post_ffn_w, wup_b, wdn_b,
              *, T, ns, fc):
    B, L, D = x.shape
    grid = (B // ns * (L // T), D_FF // fc)
    nt = L // T

    def tok(i, f):
        return (i // nt, i % nt, 0)

    def seq(i, f):
        return (i // nt, 0, 0)

    def full(shape):
        return pl.BlockSpec(shape, lambda i, f: (0, 0), pipeline_mode=pl.Buffered(1))

    assert D_FF // fc >= 2 and (ns * T) % 32 == 0
    kern = functools.partial(_ffn_kernel, ns=ns, T=T, nf=D_FF // fc)
    return pl.pallas_call(
        kern,
        grid=grid,
        in_specs=[
            pl.BlockSpec((ns, T, D), tok),
            pl.BlockSpec((ns, T, D), tok),
            pl.BlockSpec((ns, 1, D), seq),
            pl.BlockSpec((ns, 1, D), seq),
            pl.BlockSpec((ns, 1, D), seq),
            pl.BlockSpec((ns, 1, D), seq),
            full((D, D)),
            full((1, D)),
            full((1, D)),
            full((1, D)),
            pl.BlockSpec((D, fc), lambda i, f: (0, f)),
            pl.BlockSpec((fc, D), lambda i, f: (f, 0)),
        ],
        out_specs=pl.BlockSpec((ns, T, D), tok),
        out_shape=jax.ShapeDtypeStruct((B, L, D), F32),
        scratch_shapes=[
            pltpu.VMEM((ns * T, D), BF16),
            pltpu.VMEM((ns * T, D), F32),
        ],
        compiler_params=pltpu.CompilerParams(
            dimension_semantics=("arbitrary", "arbitrary"), vmem_limit_bytes=VMEM_LIMIT),
        name="channel_mix",
    )(x, mix, g1, sh2, sc2, g2, wout_b, post_mix_w, pre_ffn_w, post_ffn_w, wup_b, wdn_b)


def _pick(n, pref):
    t = min(n, pref)
    while n % t:
        t //= 2
    return t


def _layer(x, mod, states, lw, ffn_ws, *, pos_base, mix_T, mix_ns, ffn_T, ffn_ns):
    (pre_mix_w, post_mix_w, pre_ffn_w, post_ffn_w, win_b, conv_w, conv_b, dtb, alog, dskip_e,
     normw, inv2) = lw
    B = x.shape[0]
    sh1, sc1, g1, sh2, sc2, g2 = [m.reshape(B, 1, D_MODEL) for m in jnp.split(mod, 6, axis=-1)]
    cast_ws = tuple(w for w in ffn_ws if w.dtype != BF16)
    res = _mix_call(
        x, sh1, sc1, pre_mix_w, win_b, inv2, conv_w, conv_b, dtb, alog, dskip_e, normw, states,
        cast_ws, T=mix_T, ns=mix_ns, pos_base=pos_base)
    mix, r_new, s_new, c_new = res[:4]
    if cast_ws:
        ffn_ws = tuple(res[4:])
    wout_b, wup_b, wdn_b = ffn_ws
    y = _ffn_call(x, mix, g1, sh2, sc2, g2, wout_b, post_mix_w, pre_ffn_w, post_ffn_w, wup_b, wdn_b,
                  T=ffn_T, ns=ffn_ns, fc=FFN_CHUNK)
    return y, r_new, s_new.reshape(B, SSM_HEADS, SSM_HEADDIM, SSM_STATE), c_new, ffn_ws


def kernel(x_prompt, x_sample, state_ret, state_ssm, state_conv, c_prompt, c_sample, w_ada, b_ada,
           pre_mix_w, post_mix_w, pre_ffn_w, post_ffn_w, w_in, conv_w, conv_b, dt_bias, a_log, d_skip,
           ssm_norm_w, w_out, w_up, w_down):
    depth = w_ada.shape[0]
    bp, lp, _ = x_prompt.shape
    bs, ls, _ = x_sample.shape
    half = RET_DIM // 2
    inv = ROPE_BASE ** (-jnp.arange(half, dtype=F32) / half)
    inv2 = jnp.concatenate([inv, inv]).reshape(1, RET_DIM)

    def pad_lanes(v):
        return jnp.pad(v.astype(F32), (0, LANES - v.shape[0])).reshape(1, LANES)

    yp, ys = x_prompt, x_sample
    outs = [[] for _ in range(6)]
    rows = bp + bs
    rows_pad = -(-rows // 8) * 8
    for l in range(depth):
        c_all = jnp.pad(jnp.concatenate([c_prompt, c_sample], axis=0), ((0, rows_pad - rows), (0, 0)))
        mod, win_b = _modulation(c_all, w_ada[l], b_ada[l], w_in[l])
        lw = (pre_mix_w[l].reshape(1, -1), post_mix_w[l].reshape(1, -1), pre_ffn_w[l].reshape(1, -1),
              post_ffn_w[l].reshape(1, -1), win_b, conv_w[l], conv_b[l].reshape(1, -1),
              pad_lanes(dt_bias[l]), pad_lanes(a_log[l]),
              jnp.repeat(d_skip[l].astype(F32), SSM_HEADDIM).reshape(1, -1),
              ssm_norm_w[l].reshape(1, -1), inv2)
        mix_T = _pick(lp, 256)
        ffn_ws = (w_out[l], w_up[l], w_down[l])
        n_steps = bp * (lp // mix_T)
        if any(w.shape[0] % (16 * n_steps) for w in ffn_ws):
            ffn_ws = tuple(w.astype(BF16) for w in ffn_ws)
        yp, r, s, c, ffn_ws = _layer(yp, mod[:bp], None, lw, ffn_ws, pos_base=0,
                                     mix_T=mix_T, mix_ns=1, ffn_T=_pick(lp, 512), ffn_ns=1)
        outs[0].append(r); outs[1].append(s); outs[2].append(c)
        ns_mix = _pick(bs, max(1, 128 // ls))
        ns_ffn = _pick(bs, max(1, 512 // ls))
        st = (state_ret[l], state_ssm[l].reshape(bs, SSM_WIDTH, SSM_STATE), state_conv[l])
        ys, r, s, c, _ = _layer(ys, mod[bp:bp + bs], st, lw, ffn_ws, pos_base=PAST_LEN,
                                mix_T=ls, mix_ns=ns_mix, ffn_T=ls, ffn_ns=ns_ffn)
        outs[3].append(r); outs[4].append(s); outs[5].append(c)
    return (yp, ys) + tuple(jnp.stack(o) for o in outs)
```

```python
import functools
import math

import jax
import jax.numpy as jnp
from jax import lax
from jax.experimental import pallas as pl
from jax.experimental.pallas import tpu as pltpu

F32 = jnp.float32
BF16 = jnp.bfloat16

D_MODEL = 2048
PAST_LEN = 4096
CHUNK = 64
RET_HEADS = 8
RET_DIM = 128
RET_WIDTH = RET_HEADS * RET_DIM
SSM_HEADS = 16
SSM_HEADDIM = 64
SSM_WIDTH = SSM_HEADS * SSM_HEADDIM
SSM_GROUPS = 2
HEADS_PER_GROUP = SSM_HEADS // SSM_GROUPS
SSM_STATE = 128
CONV_WIDTH = 4
CONV_DIM = SSM_WIDTH + 2 * SSM_GROUPS * SSM_STATE
D_FF = 4 * D_MODEL
ROPE_BASE = 10000.0
EPS = 1e-6

LANES = 128
CONV_PAD = 8
OFF_Q = 0
OFF_K = RET_WIDTH
OFF_V = 2 * RET_WIDTH
OFF_G = 3 * RET_WIDTH
OFF_Z = 4 * RET_WIDTH
OFF_XBC = 4 * RET_WIDTH + SSM_WIDTH
OFF_DT = OFF_XBC + CONV_DIM
IN_COLS = OFF_DT + SSM_HEADS
PROJ_BLOCK = 512
FFN_CHUNK = 1024
FFN_ROWS = 16

VMEM_LIMIT = 60 * 1024 * 1024


def _silu(x):
    h = 0.5 * x
    return h + h * jnp.tanh(h)


def _softplus(x):
    return jnp.maximum(x, 0.0) + jnp.log1p(jnp.exp(-jnp.abs(x)))


def _split3(x):
    hi = x.astype(BF16)
    r1 = x - hi.astype(F32)
    mid = r1.astype(BF16)
    lo = (r1 - mid.astype(F32)).astype(BF16)
    return hi, mid, lo


def _mod_kernel(c_ref, w_ref, b_ref, o_ref):
    s = _silu(c_ref[...]).astype(BF16)
    o_ref[...] = jnp.dot(s, w_ref[...].astype(BF16), preferred_element_type=F32) + b_ref[...]


def _modulation(c, w_ada, b_ada):
    rows, d = c.shape
    n = w_ada.shape[1]
    bn = 1024
    return pl.pallas_call(
        _mod_kernel,
        grid=(n // bn,),
        in_specs=[
            pl.BlockSpec((rows, d), lambda j: (0, 0)),
            pl.BlockSpec((d, bn), lambda j: (0, j)),
            pl.BlockSpec((1, bn), lambda j: (0, j)),
        ],
        out_specs=pl.BlockSpec((rows, bn), lambda j: (0, j)),
        out_shape=jax.ShapeDtypeStruct((rows, n), F32),
        compiler_params=pltpu.CompilerParams(
            dimension_semantics=("arbitrary",), vmem_limit_bytes=VMEM_LIMIT),
        name="adaln_mod",
    )(c, w_ada, b_ada.reshape(1, n))


def _mix_kernel(*refs, T, ns, cl, pos_base, has_state, n_cast):
    it = iter(refs)
    x_ref, sh_ref, sc_ref, pw_ref, win_ref, inv_ref = (next(it) for _ in range(6))
    cw_ref, cb_ref, dtb_ref, alog_ref, dsk_ref, nw_ref = (next(it) for _ in range(6))
    if has_state:
        sret_ref, sssm_ref, sconv_ref = (next(it) for _ in range(3))
    cast_in = [next(it) for _ in range(n_cast)]
    mix_ref, ret_ref, ssm_ref, conv_ref = (next(it) for _ in range(4))
    cast_out = [next(it) for _ in range(n_cast)]
    (hb_s, q_s, k_s, kd_s, v_s, gz_s, ext_s, act_s, dt_s, ht_s,
     mask_s, qdec_s, kdec_s) = (next(it) for _ in range(13))

    b_idx = pl.program_id(0)
    t_idx = pl.program_id(1)
    R = ns * T
    lg = [math.log(1.0 - 2.0 ** (-5.0 - h)) for h in range(RET_HEADS)]

    for wi, wo in zip(cast_in, cast_out):
        wo[...] = wi[...].astype(BF16)

    @pl.when((b_idx == 0) & (t_idx == 0))
    def _tables():
        ii = lax.broadcasted_iota(jnp.int32, (T, T), 0)
        jj = lax.broadcasted_iota(jnp.int32, (T, T), 1)
        dist = jnp.abs(ii - jj).astype(F32)
        shift = int(math.log2(cl))
        visible = (jj >> shift) <= (ii >> shift)
        ri = lax.broadcasted_iota(jnp.int32, (T, LANES), 0).astype(F32)
        for h in range(RET_HEADS):
            mask_s[h] = jnp.where(visible, jnp.exp(lg[h] * dist), 0.0)
            qdec_s[h] = jnp.exp(lg[h] * (ri + 1.0))
            kdec_s[h] = jnp.exp(lg[h] * (T - 1.0 - ri))

    @pl.when(t_idx == 0)
    def _init():
        if has_state:
            ret_ref[...] = sret_ref[...]
            for s in range(ns):
                for pair in range(SSM_HEADS // 2):
                    pc = slice(pair * LANES, (pair + 1) * LANES)
                    ht_s[s, :, pc] = sssm_ref[s, pc, :].T
        else:
            ret_ref[...] = jnp.zeros(ret_ref.shape, F32)
            ht_s[...] = jnp.zeros(ht_s.shape, F32)
        ext_s[:, 0:CONV_PAD, :] = jnp.zeros((ns, CONV_PAD, CONV_DIM), F32)
        dt_s[...] = jnp.zeros(dt_s.shape, F32)
        if has_state:
            ext_s[:, CONV_PAD - (CONV_WIDTH - 1):CONV_PAD, :] = sconv_ref[...]

    pj = {}

    def proj_prologue():
        scale = pw_ref[...] * (1.0 + sc_ref[...])
        rc = min(T, 64)
        for r0 in range(0, T, rc):
            x3 = x_ref[:, r0:r0 + rc, :]
            ms = jnp.mean(x3 * x3, axis=-1, keepdims=True)
            hmod = x3 * lax.rsqrt(ms + EPS) * scale + sh_ref[...]
            for s in range(ns):
                hb_s[s * T + r0:s * T + r0 + rc, :] = hmod[s].astype(BF16)

    def rotary_tables():
        hr = R // 2
        row = lax.broadcasted_iota(jnp.int32, (hr, LANES), 0)
        low = lax.broadcasted_iota(jnp.int32, (hr, LANES), 1) < RET_DIM // 2
        row = jnp.where(low, row, row + hr)
        pos = (pos_base + t_idx * T + (row & (T - 1))).astype(F32)
        ang = pos * inv_ref[...]
        c2, s2 = jnp.cos(ang), jnp.sin(ang)
        c2r, s2r = pltpu.roll(c2, RET_DIM // 2, 1), pltpu.roll(s2, RET_DIM // 2, 1)
        pj["cos"] = jnp.concatenate([jnp.where(low, c2, c2r), jnp.where(low, c2r, c2)], axis=0)
        pj["sin"] = jnp.concatenate([jnp.where(low, -s2, s2r), jnp.where(low, -s2r, s2)], axis=0)

    def rot(u):
        return u * pj["cos"] + pltpu.roll(u, RET_DIM // 2, 1) * pj["sin"]

    def put(dst, c0, val):
        w = val.shape[1]
        for s in range(ns):
            dst[s,:, c0:c0 + w] = val[s * T:(s + 1) * T]

    def proj_block(c0):
        res = jnp.dot(hb_s[...], win_ref[:, c0:c0 + PROJ_BLOCK], preferred_element_type=F32)
        if c0 < OFF_V:
            for j in range(PROJ_BLOCK // RET_DIM):
                cj = c0 + j * RET_DIM
                r = rot(res[:, j * RET_DIM:(j + 1) * RET_DIM])
                if c0 < OFF_K:
                    put(q_s, cj, r.astype(BF16))
                else:
                    kf = r * (RET_DIM ** -0.5)
                    h = (cj - OFF_K) // RET_DIM
                    put(k_s, cj - OFF_K, kf.astype(BF16))
                    for s in range(ns):
                        kd_s[s,:, cj - OFF_K:cj - OFF_K + RET_DIM] = (
                            kf[s * T:(s + 1) * T] * kdec_s[h]).astype(BF16)
        elif c0 < OFF_G:
            put(v_s, c0 - OFF_V, res.astype(BF16))
        elif c0 < OFF_XBC:
            put(gz_s, c0 - OFF_G, _silu(res))
        else:
            for s in range(ns):
                ext_s[s,CONV_PAD:CONV_PAD + T, c0 - OFF_XBC:c0 - OFF_XBC + PROJ_BLOCK] = (
                    res[s * T:(s + 1) * T])

    def proj_dt():
        res = jnp.dot(hb_s[...], win_ref[:, OFF_DT:IN_COLS], preferred_element_type=F32)
        put(dt_s, 0, _softplus(res + dtb_ref[:, 0:SSM_HEADS]))

    proj_units = [proj_prologue]
    proj_units += [functools.partial(proj_block, c0) for c0 in range(0, OFF_DT, PROJ_BLOCK)]
    proj_units += [proj_dt]

    def mixer_units(s, slot=0):
        mx = {}
        units = []

        def consts():
            ti = lax.broadcasted_iota(jnp.int32, (T, T), 0)
            tj = lax.broadcasted_iota(jnp.int32, (T, T), 1)
            mx["causal"] = tj <= ti
            mx["tri"] = mx["causal"].astype(BF16)
            mx["low"] = lax.broadcasted_iota(jnp.int32, (T, LANES), 1) < SSM_HEADDIM
        units.append(consts)

        def ret_head(h):
            cs = slice(h * RET_DIM, (h + 1) * RET_DIM)
            qh = q_s[s,:, cs]
            kh = k_s[s,:, cs]
            kdh = kd_s[s,:, cs]
            vh = v_s[s,:, cs]
            S_old = ret_ref[s, h]
            sc = lax.dot_general(qh, kh, (((1,), (1,)), ((), ())), preferred_element_type=F32)
            p = (sc * mask_s[h]).astype(BF16)
            o = jnp.dot(p, vh, preferred_element_type=F32)
            o = o + jnp.dot(qh, S_old.astype(BF16), preferred_element_type=F32) * qdec_s[h]
            upd = lax.dot_general(kdh, vh, (((0,), (0,)), ((), ())), preferred_element_type=F32)
            ret_ref[s, h] = math.exp(lg[h] * T) * S_old + upd
            mu = jnp.mean(o, axis=-1, keepdims=True)
            d = o - mu
            var = jnp.mean(d * d, axis=-1, keepdims=True)
            mix_ref[s, :, cs] = (d * lax.rsqrt(var + EPS) * gz_s[s,:, cs]).astype(BF16)
        units += [functools.partial(ret_head, h) for h in range(RET_HEADS)]

        def conv_block(c0):
            cc = slice(c0, c0 + PROJ_BLOCK)
            base = CONV_PAD - (CONV_WIDTH - 1)
            acc = cb_ref[:, cc] + cw_ref[0:1, cc] * ext_s[s,base:base + T, cc]
            for j in range(1, CONV_WIDTH):
                acc = acc + cw_ref[j:j + 1, cc] * ext_s[s,base + j:base + j + T, cc]
            act_s[slot, :,cc] = _silu(acc)
        units += [functools.partial(conv_block, c0) for c0 in range(0, CONV_DIM, PROJ_BLOCK)]

        def conv_tail():
            conv_ref[s] = ext_s[s,CONV_PAD + T - (CONV_WIDTH - 1):CONV_PAD + T, :]
            ext_s[s,0:CONV_PAD, :] = ext_s[s,T:T + CONV_PAD, :]

        def ssd_cum():
            conv_tail()
            dt = dt_s[s]
            a = dt * (-jnp.exp(alog_ref[...]))
            a_hi, a_mid, a_lo = _split3(a)
            tri = mx["tri"]
            cum = (jnp.dot(tri, a_hi, preferred_element_type=F32)
                   + jnp.dot(tri, a_mid, preferred_element_type=F32)
                   + jnp.dot(tri, a_lo, preferred_element_type=F32))
            mx["cum"] = cum
            mx["cum_t"] = cum.T
            mx["dt_t"] = dt.T
        units.append(ssd_cum)

        gcols = HEADS_PER_GROUP * SSM_HEADDIM

        def ssd_group(g):
            b_f = act_s[slot, :,SSM_WIDTH + g * SSM_STATE:SSM_WIDTH + (g + 1) * SSM_STATE]
            b_g = b_f.astype(BF16)
            c_g = act_s[slot, :,SSM_WIDTH + (SSM_GROUPS + g) * SSM_STATE:
                        SSM_WIDTH + (SSM_GROUPS + g + 1) * SSM_STATE].astype(BF16)
            mx["b_t"] = b_f.T
            mx["gmat"] = lax.dot_general(c_g, b_g, (((1,), (1,)), ((), ())),
                                         preferred_element_type=F32)
            h_old = ht_s[s, :, g * gcols:(g + 1) * gcols]
            mx["y_int"] = jnp.dot(c_g, h_old.astype(BF16), preferred_element_type=F32)

        def ssd_pair(g, m):
            causal, low = mx["causal"], mx["low"]
            cum, cum_t, dt_t = mx["cum"], mx["cum_t"], mx["dt_t"]
            pair = g * (HEADS_PER_GROUP // 2) + m
            h0 = 2 * pair
            pc = slice(pair * LANES, (pair + 1) * LANES)
            ws, bcs, es, decs = [], [], [], []
            for hh in (h0, h0 + 1):
                cbh = jnp.broadcast_to(cum[:, hh:hh + 1], (T, LANES))
                if T < LANES:
                    decay = (jnp.where(causal, jnp.exp(cbh[:, :T] - cum_t[hh:hh + 1, :]), 0.0)
                             * dt_t[hh:hh + 1, :])
                    ws.append((decay * mx["gmat"]).astype(BF16))
                    coef = decay[T - 1:T, :]
                else:
                    nb = T // LANES
                    diag = causal[0:LANES, 0:LANES]
                    wcols, coefs = [], []
                    for c in range(nb):
                        cc = slice(c * LANES, (c + 1) * LANES)
                        wrows = [jnp.zeros((c * LANES, LANES), BF16)] if c else []
                        for r in range(c, nb):
                            rr = slice(r * LANES, (r + 1) * LANES)
                            e = jnp.exp(cbh[rr] - cum_t[hh:hh + 1, cc])
                            d = (jnp.where(diag, e, 0.0) if r == c else e) * dt_t[hh:hh + 1, cc]
                            wrows.append((d * mx["gmat"][rr, cc]).astype(BF16))
                        wcols.append(jnp.concatenate(wrows, axis=0))
                        coefs.append(d[LANES - 1:LANES, :])
                    ws.append(jnp.concatenate(wcols, axis=1))
                    coef = jnp.concatenate(coefs, axis=1)
                bcs.append((mx["b_t"] * coef).astype(BF16))
                es.append(jnp.exp(cbh))
                decs.append(jnp.exp(cum_t[hh:hh + 1, T - 1:T]))
            xs_p = act_s[slot, :,pc]
            xs_b = xs_p.astype(BF16)
            zero = jnp.zeros_like(xs_b)
            rhs = jnp.concatenate([jnp.where(low, xs_b, zero),
                                   jnp.where(low, zero, xs_b)], axis=0)
            y = jnp.dot(jnp.concatenate(ws, axis=1), rhs, preferred_element_type=F32)
            y = (y + mx["y_int"][:, m * LANES:(m + 1) * LANES] * jnp.where(low, es[0], es[1])
                 + dsk_ref[:, pc] * xs_p)
            act_s[slot, :,pc] = y
            upd = jnp.dot(jnp.concatenate(bcs, axis=1), rhs, preferred_element_type=F32)
            low_n = lax.broadcasted_iota(jnp.int32, (SSM_STATE, LANES), 1) < SSM_HEADDIM
            ht_s[s, :, pc] = jnp.where(low_n, decs[0], decs[1]) * ht_s[s, :, pc] + upd

        def ssd_state(g):
            for m in range(HEADS_PER_GROUP // 2):
                pair = g * (HEADS_PER_GROUP // 2) + m
                pc = slice(pair * LANES, (pair + 1) * LANES)
                ssm_ref[s, pc, :] = ht_s[s, :, pc].T

        for g in range(SSM_GROUPS):
            units.append(functools.partial(ssd_group, g))
            units += [functools.partial(ssd_pair, g, m) for m in range(HEADS_PER_GROUP // 2)]
            units.append(functools.partial(ssd_state, g))

        def ssd_norm(g):
            gw = SSM_WIDTH // SSM_GROUPS
            gc = slice(g * gw, (g + 1) * gw)
            yz = act_s[slot, :,gc] * gz_s[s,:, RET_WIDTH + g * gw:RET_WIDTH + (g + 1) * gw]
            msq = jnp.mean(yz * yz, axis=-1, keepdims=True)
            mix_ref[s, :, RET_WIDTH + g * gw:RET_WIDTH + (g + 1) * gw] = (
                yz * lax.rsqrt(msq + EPS) * nw_ref[:, gc]).astype(BF16)
        units += [functools.partial(ssd_norm, g) for g in range(SSM_GROUPS)]
        return units

    if ns == 1:
        blk = {c0: u for c0, u in zip(range(0, OFF_DT, PROJ_BLOCK), proj_units[1:-1])}
        mu = mixer_units(0)
        proj_units[0]()
        mu[0]()
        for c0 in range(OFF_XBC, OFF_DT, PROJ_BLOCK):
            blk[c0]()
        proj_units[-1]()
        rotary_tables()
        order = list(range(OFF_Q, OFF_XBC, PROJ_BLOCK))
        after = [[9], [10], [11], [12, 13], [14, 15], [16, 17, 18], [19, 20, 21], [22, 23, 24],
                 [1, 2, 3, 4], [25, 26, 5, 6, 7, 8]]
        assert len(order) == len(after) and sorted(sum(after, [0])) == list(range(len(mu)))
        for c0, bis in zip(order, after):
            blk[c0]()
            for bi in bis:
                mu[bi]()
    else:
        rotary_tables()
        for au in proj_units:
            au()

        unroll = act_s.shape[0]

        def seq_body(i, carry):
            streams = [mixer_units(i * unroll + u, u) for u in range(unroll)]
            for step_units in zip(*streams):
                for bu in step_units:
                    bu()
            return carry

        lax.fori_loop(0, ns // unroll, seq_body, 0)


def _mix_call(x, sh1, sc1, pre_w, win_b, inv2, conv_w, conv_b, dtb, alog, dskip_e, normw,
              states, cast_ws, *, T, ns, pos_base):
    B, L, D = x.shape
    has_state = states is not None
    cl = min(L, CHUNK)
    n_t = L // T
    grid = (B // ns, n_t)
    n_steps = grid[0] * grid[1]
    R = ns * T
    unroll = math.gcd(ns, 4)

    def full(shape):
        nd = len(shape)
        return pl.BlockSpec(shape, lambda b, t, _nd=nd: (0,) * _nd, pipeline_mode=pl.Buffered(1))

    in_specs = [
        pl.BlockSpec((ns, T, D), lambda b, t: (b, t, 0)),
        pl.BlockSpec((ns, 1, D), lambda b, t: (b, 0, 0)),
        pl.BlockSpec((ns, 1, D), lambda b, t: (b, 0, 0)),
        full((1, D)),
        full(win_b.shape),
        full((1, LANES)),
        full(conv_w.shape),
        full((1, CONV_DIM)),
        full((1, LANES)),
        full((1, LANES)),
        full((1, SSM_WIDTH)),
        full((1, SSM_WIDTH)),
    ]
    args = [x, sh1, sc1, pre_w, win_b, inv2, conv_w, conv_b, dtb, alog, dskip_e, normw]
    if has_state:
        in_specs += [
            pl.BlockSpec((ns, RET_HEADS, RET_DIM, RET_DIM), lambda b, t: (b, 0, 0, 0)),
            pl.BlockSpec((ns, SSM_WIDTH, SSM_STATE), lambda b, t: (b, 0, 0)),
            pl.BlockSpec((ns, CONV_WIDTH - 1, CONV_DIM), lambda b, t: (b, 0, 0)),
        ]
        args += list(states)
    cast_specs = []
    for w in cast_ws:
        rows = w.shape[0] // n_steps
        assert rows * n_steps == w.shape[0] and rows % 16 == 0
        cast_specs.append(pl.BlockSpec((rows, w.shape[1]), lambda b, t: (b * n_t + t, 0)))
    in_specs += cast_specs
    args += list(cast_ws)
    out_specs = [
        pl.BlockSpec((ns, T, D), lambda b, t: (b, t, 0)),
        pl.BlockSpec((ns, RET_HEADS, RET_DIM, RET_DIM), lambda b, t: (b, 0, 0, 0)),
        pl.BlockSpec((ns, SSM_WIDTH, SSM_STATE), lambda b, t: (b, 0, 0)),
        pl.BlockSpec((ns, CONV_WIDTH - 1, CONV_DIM), lambda b, t: (b, 0, 0)),
    ]
    out_shape = [
        jax.ShapeDtypeStruct((B, L, D), BF16),
        jax.ShapeDtypeStruct((B, RET_HEADS, RET_DIM, RET_DIM), F32),
        jax.ShapeDtypeStruct((B, SSM_WIDTH, SSM_STATE), F32),
        jax.ShapeDtypeStruct((B, CONV_WIDTH - 1, CONV_DIM), F32),
    ]
    out_specs += cast_specs
    out_shape += [jax.ShapeDtypeStruct(w.shape, BF16) for w in cast_ws]
    scratch = [
        pltpu.VMEM((R, D), BF16),
        pltpu.VMEM((ns, T, RET_WIDTH), BF16),
        pltpu.VMEM((ns, T, RET_WIDTH), BF16),
        pltpu.VMEM((ns, T, RET_WIDTH), BF16),
        pltpu.VMEM((ns, T, RET_WIDTH), BF16),
        pltpu.VMEM((ns, T, 2 * RET_WIDTH), F32),
        pltpu.VMEM((ns, T + CONV_PAD, CONV_DIM), F32),
        pltpu.VMEM((unroll, T, CONV_DIM), F32),
        pltpu.VMEM((ns, T, LANES), F32),
        pltpu.VMEM((ns, SSM_STATE, SSM_WIDTH), F32),
        pltpu.VMEM((RET_HEADS, T, T), F32),
        pltpu.VMEM((RET_HEADS, T, LANES), F32),
        pltpu.VMEM((RET_HEADS, T, LANES), F32),
    ]
    kern = functools.partial(_mix_kernel, T=T, ns=ns, cl=cl, pos_base=pos_base, has_state=has_state,
                             n_cast=len(cast_ws))
    return pl.pallas_call(
        kern,
        grid=grid,
        in_specs=in_specs,
        out_specs=out_specs,
        out_shape=out_shape,
        scratch_shapes=scratch,
        compiler_params=pltpu.CompilerParams(
            dimension_semantics=("arbitrary", "arbitrary"), vmem_limit_bytes=VMEM_LIMIT),
        name="token_mix_state" if has_state else "token_mix",
    )(*args)


def _ffn_kernel(x_ref, mix_ref, g1_ref, sh_ref, sc_ref, g2_ref, wout_ref, pmw_ref, pfw_ref, qfw_ref,
                wup_ref, wdn_ref, o_ref, h2_s, acc_s, *, ns, T, nf):
    f = pl.program_id(1)
    R = ns * T
    rc = min(T, FFN_ROWS)
    halves = ((0, R // 2), (R // 2, R))

    def inv_rms(v):
        return lax.rsqrt(jnp.mean(v * v, axis=-1, keepdims=True) + EPS)

    def passes(lo, hi):
        return [(s, r0) for s in range(ns) for r0 in range(0, T, rc) if lo <= s * T + r0 < hi]

    def out_proj(lo, hi):
        lhs = mix_ref[0, lo:hi, :] if ns == 1 else mix_ref[lo // T:hi // T].reshape(hi - lo, D_MODEL)
        acc_s[lo:hi, :] = jnp.dot(lhs, wout_ref[...], preferred_element_type=F32)

    def first_norms(lo, hi):
        for s, r0 in passes(lo, hi):
            rows = slice(s * T + r0, s * T + r0 + rc)
            m = acc_s[rows, :]
            x1 = x_ref[s, r0:r0 + rc, :] + m * inv_rms(m) * (pmw_ref[...] * g1_ref[s])
            o_ref[s, r0:r0 + rc, :] = x1
            h2_s[rows, :] = (x1 * inv_rms(x1) * (pfw_ref[...] * (1.0 + sc_ref[s])) + sh_ref[s]).astype(BF16)
            acc_s[rows, :] = jnp.zeros((rc, D_MODEL), F32)

    def mlp(lo, hi):
        u = jnp.dot(h2_s[lo:hi, :], wup_ref[...], preferred_element_type=F32)
        u = jnp.maximum(u, 0.0)
        u = (u * u).astype(BF16)
        acc_s[lo:hi, :] += jnp.dot(u, wdn_ref[...], preferred_element_type=F32)

    def last_norms(lo, hi):
        for s, r0 in passes(lo, hi):
            a = acc_s[s * T + r0:s * T + r0 + rc, :]
            o_ref[s, r0:r0 + rc, :] = (o_ref[s, r0:r0 + rc, :]
                                       + a * inv_rms(a) * (qfw_ref[...] * g2_ref[s]))

    @pl.when(f == 0)
    def _first():
        for lo, hi in halves:
            out_proj(lo, hi)
        for lo, hi in halves:
            first_norms(lo, hi)
            mlp(lo, hi)

    @pl.when((f > 0) & (f < nf - 1))
    def _middle():
        mlp(0, R)

    @pl.when(f == nf - 1)
    def _last():
        for lo, hi in halves:
            mlp(lo, hi)
        for lo, hi in halves:
            last_norms(lo, hi)


def _ffn_call(x, mix, g1, sh2, sc2, g2, wout_b, post_mix_w, pre_ffn_w, post_ffn_w, wup_b, wdn_b,
              *, T, ns, fc):
    B, L, D = x.shape
    grid = (B // ns * (L // T), D_FF // fc)
    nt = L // T

    def tok(i, f):
        return (i // nt, i % nt, 0)

    def seq(i, f):
        return (i // nt, 0, 0)

    def full(shape):
        return pl.BlockSpec(shape, lambda i, f: (0, 0), pipeline_mode=pl.Buffered(1))

    assert D_FF // fc >= 2 and (ns * T) % 32 == 0
    kern = functools.partial(_ffn_kernel, ns=ns, T=T, nf=D_FF // fc)
    return pl.pallas_call(
        kern,
        grid=grid,
        in_specs=[
            pl.BlockSpec((ns, T, D), tok),
            pl.BlockSpec((ns, T, D), tok),
            pl.BlockSpec((ns, 1, D), seq),
            pl.BlockSpec((ns, 1, D), seq),
            pl.BlockSpec((ns, 1, D), seq),
            pl.BlockSpec((ns, 1, D), seq),
            full((D, D)),
            full((1, D)),
            full((1, D)),
            full((1, D)),
            pl.BlockSpec((D, fc), lambda i, f: (0, f)),
            pl.BlockSpec((fc, D), lambda i, f: (f, 0)),
        ],
        out_specs=pl.BlockSpec((ns, T, D), tok),
        out_shape=jax.ShapeDtypeStruct((B, L, D), F32),
        scratch_shapes=[
            pltpu.VMEM((ns * T, D), BF16),
            pltpu.VMEM((ns * T, D), F32),
        ],
        compiler_params=pltpu.CompilerParams(
            dimension_semantics=("arbitrary", "arbitrary"), vmem_limit_bytes=VMEM_LIMIT),
        name="channel_mix",
    )(x, mix, g1, sh2, sc2, g2, wout_b, post_mix_w, pre_ffn_w, post_ffn_w, wup_b, wdn_b)


def _pick(n, pref):
    t = min(n, pref)
    while n % t:
        t //= 2
    return t


def _layer(x, mod, states, lw, ffn_ws, *, pos_base, mix_T, mix_ns, ffn_T, ffn_ns):
    (pre_mix_w, post_mix_w, pre_ffn_w, post_ffn_w, win_b, conv_w, conv_b, dtb, alog, dskip_e,
     normw, inv2) = lw
    B = x.shape[0]
    sh1, sc1, g1, sh2, sc2, g2 = [m.reshape(B, 1, D_MODEL) for m in jnp.split(mod, 6, axis=-1)]
    cast_ws = tuple(w for w in ffn_ws if w.dtype != BF16)
    res = _mix_call(
        x, sh1, sc1, pre_mix_w, win_b, inv2, conv_w, conv_b, dtb, alog, dskip_e, normw, states,
        cast_ws, T=mix_T, ns=mix_ns, pos_base=pos_base)
    mix, r_new, s_new, c_new = res[:4]
    if cast_ws:
        ffn_ws = tuple(res[4:])
    wout_b, wup_b, wdn_b = ffn_ws
    y = _ffn_call(x, mix, g1, sh2, sc2, g2, wout_b, post_mix_w, pre_ffn_w, post_ffn_w, wup_b, wdn_b,
                  T=ffn_T, ns=ffn_ns, fc=FFN_CHUNK)
    return y, r_new, s_new.reshape(B, SSM_HEADS, SSM_HEADDIM, SSM_STATE), c_new, ffn_ws


def kernel(x_prompt, x_sample, state_ret, state_ssm, state_conv, c_prompt, c_sample, w_ada, b_ada,
           pre_mix_w, post_mix_w, pre_ffn_w, post_ffn_w, w_in, conv_w, conv_b, dt_bias, a_log, d_skip,
           ssm_norm_w, w_out, w_up, w_down):
    depth = w_ada.shape[0]
    bp, lp, _ = x_prompt.shape
    bs, ls, _ = x_sample.shape
    half = RET_DIM // 2
    inv = ROPE_BASE ** (-jnp.arange(half, dtype=F32) / half)
    inv2 = jnp.concatenate([inv, inv]).reshape(1, RET_DIM)

    def pad_lanes(v):
        return jnp.pad(v.astype(F32), (0, LANES - v.shape[0])).reshape(1, LANES)

    yp, ys = x_prompt, x_sample
    outs = [[] for _ in range(6)]
    rows = bp + bs
    rows_pad = -(-rows // 8) * 8
    for l in range(depth):
        c_all = jnp.pad(jnp.concatenate([c_prompt, c_sample], axis=0), ((0, rows_pad - rows), (0, 0)))
        mod = _modulation(c_all, w_ada[l], b_ada[l])
        win_b = w_in[l].astype(BF16)
        lw = (pre_mix_w[l].reshape(1, -1), post_mix_w[l].reshape(1, -1), pre_ffn_w[l].reshape(1, -1),
              post_ffn_w[l].reshape(1, -1), win_b, conv_w[l], conv_b[l].reshape(1, -1),
              pad_lanes(dt_bias[l]), pad_lanes(a_log[l]),
              jnp.repeat(d_skip[l].astype(F32), SSM_HEADDIM).reshape(1, -1),
              ssm_norm_w[l].reshape(1, -1), inv2)
        mix_T = _pick(lp, 256)
        ffn_ws = (w_out[l], w_up[l], w_down[l])
        n_steps = bp * (lp // mix_T)
        if any(w.shape[0] % (16 * n_steps) for w in ffn_ws):
            ffn_ws = tuple(w.astype(BF16) for w in ffn_ws)
        yp, r, s, c, ffn_ws = _layer(yp, mod[:bp], None, lw, ffn_ws, pos_base=0,
                                     mix_T=mix_T, mix_ns=1, ffn_T=_pick(lp, 512), ffn_ns=1)
        outs[0].append(r); outs[1].append(s); outs[2].append(c)
        ns_mix = _pick(bs, max(1, 128 // ls))
        ns_ffn = _pick(bs, max(1, 512 // ls))
        st = (state_ret[l], state_ssm[l].reshape(bs, SSM_WIDTH, SSM_STATE), state_conv[l])
        ys, r, s, c, _ = _layer(ys, mod[bp:bp + bs], st, lw, ffn_ws, pos_base=PAST_LEN,
                                mix_T=ls, mix_ns=ns_mix, ffn_T=ls, ffn_ns=ns_ffn)
        outs[3].append(r); outs[4].append(s); outs[5].append(c)
    return (yp, ys) + tuple(jnp.stack(o) for o in outs)
```

```python
import functools
import math

import jax
import jax.numpy as jnp
from jax import lax
from jax.experimental import pallas as pl
from jax.experimental.pallas import tpu as pltpu

F32 = jnp.float32
BF16 = jnp.bfloat16

D_MODEL = 2048
PAST_LEN = 4096
CHUNK = 64
RET_HEADS = 8
RET_DIM = 128
RET_WIDTH = RET_HEADS * RET_DIM
SSM_HEADS = 16
SSM_HEADDIM = 64
SSM_WIDTH = SSM_HEADS * SSM_HEADDIM
SSM_GROUPS = 2
HEADS_PER_GROUP = SSM_HEADS // SSM_GROUPS
SSM_STATE = 128
CONV_WIDTH = 4
CONV_DIM = SSM_WIDTH + 2 * SSM_GROUPS * SSM_STATE
D_FF = 4 * D_MODEL
ROPE_BASE = 10000.0
EPS = 1e-6

LANES = 128
CONV_PAD = 8
OFF_Q = 0
OFF_K = RET_WIDTH
OFF_V = 2 * RET_WIDTH
OFF_G = 3 * RET_WIDTH
OFF_Z = 4 * RET_WIDTH
OFF_XBC = 4 * RET_WIDTH + SSM_WIDTH
OFF_DT = OFF_XBC + CONV_DIM
IN_COLS = OFF_DT + SSM_HEADS
PROJ_BLOCK = 512
FFN_CHUNK = 1024
FFN_ROWS = 16

VMEM_LIMIT = 60 * 1024 * 1024


def _silu(x):
    h = 0.5 * x
    return h + h * jnp.tanh(h)


def _softplus(x):
    return jnp.maximum(x, 0.0) + jnp.log1p(jnp.exp(-jnp.abs(x)))


def _split3(x):
    hi = x.astype(BF16)
    r1 = x - hi.astype(F32)
    mid = r1.astype(BF16)
    lo = (r1 - mid.astype(F32)).astype(BF16)
    return hi, mid, lo


def _mod_kernel(c_ref, w_ref, b_ref, o_ref):
    s = _silu(c_ref[...]).astype(BF16)
    o_ref[...] = jnp.dot(s, w_ref[...].astype(BF16), preferred_element_type=F32) + b_ref[...]


def _modulation(c, w_ada, b_ada):
    rows, d = c.shape
    n = w_ada.shape[1]
    bn = 1024
    return pl.pallas_call(
        _mod_kernel,
        grid=(n // bn,),
        in_specs=[
            pl.BlockSpec((rows, d), lambda j: (0, 0)),
            pl.BlockSpec((d, bn), lambda j: (0, j)),
            pl.BlockSpec((1, bn), lambda j: (0, j)),
        ],
        out_specs=pl.BlockSpec((rows, bn), lambda j: (0, j)),
        out_shape=jax.ShapeDtypeStruct((rows, n), F32),
        compiler_params=pltpu.CompilerParams(
            dimension_semantics=("arbitrary",), vmem_limit_bytes=VMEM_LIMIT),
        name="adaln_mod",
    )(c, w_ada, b_ada.reshape(1, n))


def _mix_kernel(*refs, T, ns, cl, pos_base, has_state, n_cast):
    it = iter(refs)
    x_ref, sh_ref, sc_ref, pw_ref, win_ref, inv_ref = (next(it) for _ in range(6))
    cw_ref, cb_ref, dtb_ref, alog_ref, dsk_ref, nw_ref = (next(it) for _ in range(6))
    if has_state:
        sret_ref, sssm_ref, sconv_ref = (next(it) for _ in range(3))
    cast_in = [next(it) for _ in range(n_cast)]
    mix_ref, ret_ref, ssm_ref, conv_ref = (next(it) for _ in range(4))
    cast_out = [next(it) for _ in range(n_cast)]
    (hb_s, q_s, k_s, kd_s, v_s, gz_s, ext_s, act_s, dt_s, ht_s,
     mask_s, qdec_s, kdec_s) = (next(it) for _ in range(13))

    b_idx = pl.program_id(0)
    t_idx = pl.program_id(1)
    R = ns * T
    lg = [math.log(1.0 - 2.0 ** (-5.0 - h)) for h in range(RET_HEADS)]

    for wi, wo in zip(cast_in, cast_out):
        wo[...] = wi[...].astype(BF16)

    @pl.when((b_idx == 0) & (t_idx == 0))
    def _tables():
        ii = lax.broadcasted_iota(jnp.int32, (T, T), 0)
        jj = lax.broadcasted_iota(jnp.int32, (T, T), 1)
        dist = jnp.abs(ii - jj).astype(F32)
        shift = int(math.log2(cl))
        visible = (jj >> shift) <= (ii >> shift)
        ri = lax.broadcasted_iota(jnp.int32, (T, LANES), 0).astype(F32)
        for h in range(RET_HEADS):
            mask_s[h] = jnp.where(visible, jnp.exp(lg[h] * dist), 0.0)
            qdec_s[h] = jnp.exp(lg[h] * (ri + 1.0))
            kdec_s[h] = jnp.exp(lg[h] * (T - 1.0 - ri))

    @pl.when(t_idx == 0)
    def _init():
        if has_state:
            ret_ref[...] = sret_ref[...]
            for s in range(ns):
                for pair in range(SSM_HEADS // 2):
                    pc = slice(pair * LANES, (pair + 1) * LANES)
                    ht_s[s, :, pc] = sssm_ref[s, pc, :].T
        else:
            ret_ref[...] = jnp.zeros(ret_ref.shape, F32)
            ht_s[...] = jnp.zeros(ht_s.shape, F32)
        ext_s[:, 0:CONV_PAD, :] = jnp.zeros((ns, CONV_PAD, CONV_DIM), F32)
        dt_s[...] = jnp.zeros(dt_s.shape, F32)
        if has_state:
            ext_s[:, CONV_PAD - (CONV_WIDTH - 1):CONV_PAD, :] = sconv_ref[...]

    pj = {}

    def proj_prologue():
        scale = pw_ref[...] * (1.0 + sc_ref[...])
        rc = min(T, 64)
        for r0 in range(0, T, rc):
            x3 = x_ref[:, r0:r0 + rc, :]
            ms = jnp.mean(x3 * x3, axis=-1, keepdims=True)
            hmod = x3 * lax.rsqrt(ms + EPS) * scale + sh_ref[...]
            for s in range(ns):
                hb_s[s * T + r0:s * T + r0 + rc, :] = hmod[s].astype(BF16)

    def rotary_tables():
        hr = R // 2
        row = lax.broadcasted_iota(jnp.int32, (hr, LANES), 0)
        low = lax.broadcasted_iota(jnp.int32, (hr, LANES), 1) < RET_DIM // 2
        row = jnp.where(low, row, row + hr)
        pos = (pos_base + t_idx * T + (row & (T - 1))).astype(F32)
        ang = pos * inv_ref[...]
        c2, s2 = jnp.cos(ang), jnp.sin(ang)
        c2r, s2r = pltpu.roll(c2, RET_DIM // 2, 1), pltpu.roll(s2, RET_DIM // 2, 1)
        pj["cos"] = jnp.concatenate([jnp.where(low, c2, c2r), jnp.where(low, c2r, c2)], axis=0)
        pj["sin"] = jnp.concatenate([jnp.where(low, -s2, s2r), jnp.where(low, -s2r, s2)], axis=0)

    def rot(u):
        return u * pj["cos"] + pltpu.roll(u, RET_DIM // 2, 1) * pj["sin"]

    def put(dst, c0, val):
        w = val.shape[1]
        for s in range(ns):
            dst[s, :, c0:c0 + w] = val[s * T:(s + 1) * T]

    def proj_block(c0):
        res = jnp.dot(hb_s[...], win_ref[:, c0:c0 + PROJ_BLOCK], preferred_element_type=F32)
        if c0 < OFF_V:
            for j in range(PROJ_BLOCK // RET_DIM):
                cj = c0 + j * RET_DIM
                r = rot(res[:, j * RET_DIM:(j + 1) * RET_DIM])
                if c0 < OFF_K:
                    put(q_s, cj, r.astype(BF16))
                else:
                    kf = r * (RET_DIM ** -0.5)
                    h = (cj - OFF_K) // RET_DIM
                    put(k_s, cj - OFF_K, kf.astype(BF16))
                    for s in range(ns):
                        kd_s[s, :, cj - OFF_K:cj - OFF_K + RET_DIM] = (
                            kf[s * T:(s + 1) * T] * kdec_s[h]).astype(BF16)
        elif c0 < OFF_G:
            put(v_s, c0 - OFF_V, res.astype(BF16))
        elif c0 < OFF_XBC:
            put(gz_s, c0 - OFF_G, _silu(res))
        else:
            for s in range(ns):
                ext_s[s, CONV_PAD:CONV_PAD + T, c0 - OFF_XBC:c0 - OFF_XBC + PROJ_BLOCK] = (
                    res[s * T:(s + 1) * T])

    def proj_dt():
        res = jnp.dot(hb_s[...], win_ref[:, OFF_DT:IN_COLS], preferred_element_type=F32)
        put(dt_s, 0, _softplus(res + dtb_ref[:, 0:SSM_HEADS]))

    proj_units = [proj_prologue]
    proj_units += [functools.partial(proj_block, c0) for c0 in range(0, OFF_DT, PROJ_BLOCK)]
    proj_units += [proj_dt]

    def mixer_units(s, slot=0):
        mx = {}
        units = []

        def consts():
            ti = lax.broadcasted_iota(jnp.int32, (T, T), 0)
            tj = lax.broadcasted_iota(jnp.int32, (T, T), 1)
            mx["causal"] = tj <= ti
            mx["tri"] = mx["causal"].astype(BF16)
            mx["low"] = lax.broadcasted_iota(jnp.int32, (T, LANES), 1) < SSM_HEADDIM
        units.append(consts)

        def ret_head(h):
            cs = slice(h * RET_DIM, (h + 1) * RET_DIM)
            qh = q_s[s, :, cs]
            kh = k_s[s, :, cs]
            kdh = kd_s[s, :, cs]
            vh = v_s[s, :, cs]
            S_old = ret_ref[s, h]
            sc = lax.dot_general(qh, kh, (((1,), (1,)), ((), ())), preferred_element_type=F32)
            p = (sc * mask_s[h]).astype(BF16)
            o = jnp.dot(p, vh, preferred_element_type=F32)
            o = o + jnp.dot(qh, S_old.astype(BF16), preferred_element_type=F32) * qdec_s[h]
            upd = lax.dot_general(kdh, vh, (((0,), (0,)), ((), ())), preferred_element_type=F32)
            ret_ref[s, h] = math.exp(lg[h] * T) * S_old + upd
            mu = jnp.mean(o, axis=-1, keepdims=True)
            d = o - mu
            var = jnp.mean(d * d, axis=-1, keepdims=True)
            mix_ref[s, :, cs] = (d * lax.rsqrt(var + EPS) * gz_s[s, :, cs]).astype(BF16)
        units += [functools.partial(ret_head, h) for h in range(RET_HEADS)]

        def conv_block(c0):
            cc = slice(c0, c0 + PROJ_BLOCK)
            base = CONV_PAD - (CONV_WIDTH - 1)
            acc = cb_ref[:, cc] + cw_ref[0:1, cc] * ext_s[s, base:base + T, cc]
            for j in range(1, CONV_WIDTH):
                acc = acc + cw_ref[j:j + 1, cc] * ext_s[s, base + j:base + j + T, cc]
            act_s[slot, :, cc] = _silu(acc)
        units += [functools.partial(conv_block, c0) for c0 in range(0, CONV_DIM, PROJ_BLOCK)]

        def conv_tail():
            conv_ref[s] = ext_s[s, CONV_PAD + T - (CONV_WIDTH - 1):CONV_PAD + T, :]
            ext_s[s, 0:CONV_PAD, :] = ext_s[s, T:T + CONV_PAD, :]

        def ssd_cum():
            conv_tail()
            dt = dt_s[s]
            a = dt * (-jnp.exp(alog_ref[...]))
            a_hi, a_mid, a_lo = _split3(a)
            tri = mx["tri"]
            cum = (jnp.dot(tri, a_hi, preferred_element_type=F32)
                   + jnp.dot(tri, a_mid, preferred_element_type=F32)
                   + jnp.dot(tri, a_lo, preferred_element_type=F32))
            mx["cum"] = cum
            mx["cum_t"] = cum.T
            mx["dt_t"] = dt.T
        units.append(ssd_cum)

        gcols = HEADS_PER_GROUP * SSM_HEADDIM

        def ssd_group(g):
            b_f = act_s[slot, :, SSM_WIDTH + g * SSM_STATE:SSM_WIDTH + (g + 1) * SSM_STATE]
            b_g = b_f.astype(BF16)
            c_g = act_s[slot, :, SSM_WIDTH + (SSM_GROUPS + g) * SSM_STATE:
                        SSM_WIDTH + (SSM_GROUPS + g + 1) * SSM_STATE].astype(BF16)
            mx["b_t"] = b_f.T
            mx["gmat"] = lax.dot_general(c_g, b_g, (((1,), (1,)), ((), ())),
                                         preferred_element_type=F32)
            h_old = ht_s[s, :, g * gcols:(g + 1) * gcols]
            mx["y_int"] = jnp.dot(c_g, h_old.astype(BF16), preferred_element_type=F32)

        def ssd_pair(g, m):
            causal, low = mx["causal"], mx["low"]
            cum, cum_t, dt_t = mx["cum"], mx["cum_t"], mx["dt_t"]
            pair = g * (HEADS_PER_GROUP // 2) + m
            h0 = 2 * pair
            pc = slice(pair * LANES, (pair + 1) * LANES)
            ws, bcs, es, decs = [], [], [], []
            for hh in (h0, h0 + 1):
                cbh = jnp.broadcast_to(cum[:, hh:hh + 1], (T, LANES))
                if T < LANES:
                    decay = (jnp.where(causal, jnp.exp(cbh[:, :T] - cum_t[hh:hh + 1, :]), 0.0)
                             * dt_t[hh:hh + 1, :])
                    ws.append((decay * mx["gmat"]).astype(BF16))
                    coef = decay[T - 1:T, :]
                else:
                    nb = T // LANES
                    diag = causal[0:LANES, 0:LANES]
                    wcols, coefs = [], []
                    for c in range(nb):
                        cc = slice(c * LANES, (c + 1) * LANES)
                        wrows = [jnp.zeros((c * LANES, LANES), BF16)] if c else []
                        for r in range(c, nb):
                            rr = slice(r * LANES, (r + 1) * LANES)
                            e = jnp.exp(cbh[rr] - cum_t[hh:hh + 1, cc])
                            d = (jnp.where(diag, e, 0.0) if r == c else e) * dt_t[hh:hh + 1, cc]
                            wrows.append((d * mx["gmat"][rr, cc]).astype(BF16))
                        wcols.append(jnp.concatenate(wrows, axis=0))
                        coefs.append(d[LANES - 1:LANES, :])
                    ws.append(jnp.concatenate(wcols, axis=1))
                    coef = jnp.concatenate(coefs, axis=1)
                bcs.append((mx["b_t"] * coef).astype(BF16))
                es.append(jnp.exp(cbh))
                decs.append(jnp.exp(cum_t[hh:hh + 1, T - 1:T]))
            xs_p = act_s[slot, :, pc]
            xs_b = xs_p.astype(BF16)
            zero = jnp.zeros_like(xs_b)
            rhs = jnp.concatenate([jnp.where(low, xs_b, zero),
                                   jnp.where(low, zero, xs_b)], axis=0)
            y = jnp.dot(jnp.concatenate(ws, axis=1), rhs, preferred_element_type=F32)
            y = (y + mx["y_int"][:, m * LANES:(m + 1) * LANES] * jnp.where(low, es[0], es[1])
                 + dsk_ref[:, pc] * xs_p)
            act_s[slot, :, pc] = y
            upd = jnp.dot(jnp.concatenate(bcs, axis=1), rhs, preferred_element_type=F32)
            low_n = lax.broadcasted_iota(jnp.int32, (SSM_STATE, LANES), 1) < SSM_HEADDIM
            ht_s[s, :, pc] = jnp.where(low_n, decs[0], decs[1]) * ht_s[s, :, pc] + upd

        def ssd_state(g):
            for m in range(HEADS_PER_GROUP // 2):
                pair = g * (HEADS_PER_GROUP // 2) + m
                pc = slice(pair * LANES, (pair + 1) * LANES)
                ssm_ref[s, pc, :] = ht_s[s, :, pc].T

        for g in range(SSM_GROUPS):
            units.append(functools.partial(ssd_group, g))
            units += [functools.partial(ssd_pair, g, m) for m in range(HEADS_PER_GROUP // 2)]
            units.append(functools.partial(ssd_state, g))

        def ssd_norm(g):
            gw = SSM_WIDTH // SSM_GROUPS
            gc = slice(g * gw, (g + 1) * gw)
            yz = act_s[slot, :, gc] * gz_s[s, :, RET_WIDTH + g * gw:RET_WIDTH + (g + 1) * gw]
            msq = jnp.mean(yz * yz, axis=-1, keepdims=True)
            mix_ref[s, :, RET_WIDTH + g * gw:RET_WIDTH + (g + 1) * gw] = (
                yz * lax.rsqrt(msq + EPS) * nw_ref[:, gc]).astype(BF16)
        units += [functools.partial(ssd_norm, g) for g in range(SSM_GROUPS)]
        return units

    if ns == 1:
        blk = {c0: u for c0, u in zip(range(0, OFF_DT, PROJ_BLOCK), proj_units[1:-1])}
        mu = mixer_units(0)
        proj_units[0]()
        mu[0]()
        for c0 in range(OFF_XBC, OFF_DT, PROJ_BLOCK):
            blk[c0]()
        proj_units[-1]()
        rotary_tables()
        order = list(range(OFF_Q, OFF_XBC, PROJ_BLOCK))
        after = [[9], [10], [11], [12, 13], [14, 15], [16, 17, 18], [19, 20, 21], [22, 23, 24],
                 [1, 2, 3, 4], [25, 26, 5, 6, 7, 8]]
        assert len(order) == len(after) and sorted(sum(after, [0])) == list(range(len(mu)))
        for c0, bis in zip(order, after):
            blk[c0]()
            for bi in bis:
                mu[bi]()
    else:
        rotary_tables()
        for au in proj_units:
            au()

        unroll = act_s.shape[0]

        def seq_body(i, carry):
            streams = [mixer_units(i * unroll + u, u) for u in range(unroll)]
            for step_units in zip(*streams):
                for bu in step_units:
                    bu()
            return carry

        lax.fori_loop(0, ns // unroll, seq_body, 0)


def _mix_call(x, sh1, sc1, pre_w, win_b, inv2, conv_w, conv_b, dtb, alog, dskip_e, normw,
              states, cast_ws, *, T, ns, pos_base):
    B, L, D = x.shape
    has_state = states is not None
    cl = min(L, CHUNK)
    n_t = L // T
    grid = (B // ns, n_t)
    n_steps = grid[0] * grid[1]
    R = ns * T
    unroll = math.gcd(ns, 4)

    def full(shape):
        nd = len(shape)
        return pl.BlockSpec(shape, lambda b, t, _nd=nd: (0,) * _nd, pipeline_mode=pl.Buffered(1))

    in_specs = [
        pl.BlockSpec((ns, T, D), lambda b, t: (b, t, 0)),
        pl.BlockSpec((ns, 1, D), lambda b, t: (b, 0, 0)),
        pl.BlockSpec((ns, 1, D), lambda b, t: (b, 0, 0)),
        full((1, D)),
        full(win_b.shape),
        full((1, LANES)),
        full(conv_w.shape),
        full((1, CONV_DIM)),
        full((1, LANES)),
        full((1, LANES)),
        full((1, SSM_WIDTH)),
        full((1, SSM_WIDTH)),
    ]
    args = [x, sh1, sc1, pre_w, win_b, inv2, conv_w, conv_b, dtb, alog, dskip_e, normw]
    if has_state:
        in_specs += [
            pl.BlockSpec((ns, RET_HEADS, RET_DIM, RET_DIM), lambda b, t: (b, 0, 0, 0)),
            pl.BlockSpec((ns, SSM_WIDTH, SSM_STATE), lambda b, t: (b, 0, 0)),
            pl.BlockSpec((ns, CONV_WIDTH - 1, CONV_DIM), lambda b, t: (b, 0, 0)),
        ]
        args += list(states)
    cast_specs = []
    for w in cast_ws:
        rows = w.shape[0] // n_steps
        assert rows * n_steps == w.shape[0] and rows % 16 == 0
        cast_specs.append(pl.BlockSpec((rows, w.shape[1]), lambda b, t: (b * n_t + t, 0)))
    in_specs += cast_specs
    args += list(cast_ws)
    out_specs = [
        pl.BlockSpec((ns, T, D), lambda b, t: (b, t, 0)),
        pl.BlockSpec((ns, RET_HEADS, RET_DIM, RET_DIM), lambda b, t: (b, 0, 0, 0)),
        pl.BlockSpec((ns, SSM_WIDTH, SSM_STATE), lambda b, t: (b, 0, 0)),
        pl.BlockSpec((ns, CONV_WIDTH - 1, CONV_DIM), lambda b, t: (b, 0, 0)),
    ]
    out_shape = [
        jax.ShapeDtypeStruct((B, L, D), BF16),
        jax.ShapeDtypeStruct((B, RET_HEADS, RET_DIM, RET_DIM), F32),
        jax.ShapeDtypeStruct((B, SSM_WIDTH, SSM_STATE), F32),
        jax.ShapeDtypeStruct((B, CONV_WIDTH - 1, CONV_DIM), F32),
    ]
    out_specs += cast_specs
    out_shape += [jax.ShapeDtypeStruct(w.shape, BF16) for w in cast_ws]
    scratch = [
        pltpu.VMEM((R, D), BF16),
        pltpu.VMEM((ns, T, RET_WIDTH), BF16),
        pltpu.VMEM((ns, T, RET_WIDTH), BF16),
        pltpu.VMEM((ns, T, RET_WIDTH), BF16),
        pltpu.VMEM((ns, T, RET_WIDTH), BF16),
        pltpu.VMEM((ns, T, 2 * RET_WIDTH), F32),
        pltpu.VMEM((ns, T + CONV_PAD, CONV_DIM), F32),
        pltpu.VMEM((unroll, T, CONV_DIM), F32),
        pltpu.VMEM((ns, T, LANES), F32),
        pltpu.VMEM((ns, SSM_STATE, SSM_WIDTH), F32),
        pltpu.VMEM((RET_HEADS, T, T), F32),
        pltpu.VMEM((RET_HEADS, T, LANES), F32),
        pltpu.VMEM((RET_HEADS, T, LANES), F32),
    ]
    kern = functools.partial(_mix_kernel, T=T, ns=ns, cl=cl, pos_base=pos_base, has_state=has_state,
                             n_cast=len(cast_ws))
    return pl.pallas_call(
        kern,
        grid=grid,
        in_specs=in_specs,
        out_specs=out_specs,
        out_shape=out_shape,
        scratch_shapes=scratch,
        compiler_params=pltpu.CompilerParams(
            dimension_semantics=("arbitrary", "arbitrary"), vmem_limit_bytes=VMEM_LIMIT),
        name="token_mix_state" if has_state else "token_mix",
    )(*args)


def _ffn_kernel(x_ref, mix_ref, g1_ref, sh_ref, sc_ref, g2_ref, wout_ref, pmw_ref, pfw_ref, qfw_ref,
                wup_ref, wdn_ref, o_ref, h2_s, acc_s, *, ns, T, nf):
    f = pl.program_id(1)
    R = ns * T
    rc = min(T, FFN_ROWS)
    halves = ((0, R // 2), (R // 2, R))

    def inv_rms(v):
        return lax.rsqrt(jnp.mean(v * v, axis=-1, keepdims=True) + EPS)

    def passes(lo, hi):
        return [(s, r0) for s in range(ns) for r0 in range(0, T, rc) if lo <= s * T + r0 < hi]

    def out_proj(lo, hi):
        lhs = mix_ref[0, lo:hi, :] if ns == 1 else mix_ref[lo // T:hi // T].reshape(hi - lo, D_MODEL)
        acc_s[lo:hi, :] = jnp.dot(lhs, wout_ref[...], preferred_element_type=F32)

    def first_norms(lo, hi):
        for s, r0 in passes(lo, hi):
            rows = slice(s * T + r0, s * T + r0 + rc)
            m = acc_s[rows, :]
            x1 = x_ref[s, r0:r0 + rc, :] + m * inv_rms(m) * (pmw_ref[...] * g1_ref[s])
            o_ref[s, r0:r0 + rc, :] = x1
            h2_s[rows, :] = (x1 * inv_rms(x1) * (pfw_ref[...] * (1.0 + sc_ref[s])) + sh_ref[s]).astype(BF16)
            acc_s[rows, :] = jnp.zeros((rc, D_MODEL), F32)

    def mlp(lo, hi):
        u = jnp.dot(h2_s[lo:hi, :], wup_ref[...], preferred_element_type=F32)
        u = jnp.maximum(u, 0.0)
        u = (u * u).astype(BF16)
        acc_s[lo:hi, :] += jnp.dot(u, wdn_ref[...], preferred_element_type=F32)

    def last_norms(lo, hi):
        for s, r0 in passes(lo, hi):
            a = acc_s[s * T + r0:s * T + r0 + rc, :]
            o_ref[s, r0:r0 + rc, :] = (o_ref[s, r0:r0 + rc, :]
                                       + a * inv_rms(a) * (qfw_ref[...] * g2_ref[s]))

    @pl.when(f == 0)
    def _first():
        for lo, hi in halves:
            out_proj(lo, hi)
        for lo, hi in halves:
            first_norms(lo, hi)
            mlp(lo, hi)

    @pl.when((f > 0) & (f < nf - 1))
    def _middle():
        mlp(0, R)

    @pl.when(f == nf - 1)
    def _last():
        for lo, hi in halves:
            mlp(lo, hi)
        for lo, hi in halves:
            last_norms(lo, hi)


def _ffn_call(x, mix, g1, sh2, sc2, g2, wout_b, post_mix_w, pre_ffn_w, post_ffn_w, wup_b, wdn_b,
              *, T, ns, fc):
    B, L, D = x.shape
    grid = (B // ns * (L // T), D_FF // fc)
    nt = L // T

    def tok(i, f):
        return (i // nt, i % nt, 0)

    def tok_in(i, f):
        j = jnp.minimum(i + jnp.minimum(f, 1), grid[0] - 1)
        return (j // nt, j % nt, 0)

    def seq(i, f):
        return (i // nt, 0, 0)

    def full(shape):
        return pl.BlockSpec(shape, lambda i, f: (0, 0), pipeline_mode=pl.Buffered(1))

    assert D_FF // fc >= 2 and (ns * T) % 32 == 0
    kern = functools.partial(_ffn_kernel, ns=ns, T=T, nf=D_FF // fc)
    return pl.pallas_call(
        kern,
        grid=grid,
        in_specs=[
            pl.BlockSpec((ns, T, D), tok_in),
            pl.BlockSpec((ns, T, D), tok_in),
            pl.BlockSpec((ns, 1, D), seq),
            pl.BlockSpec((ns, 1, D), seq),
            pl.BlockSpec((ns, 1, D), seq),
            pl.BlockSpec((ns, 1, D), seq),
            full((D, D)),
            full((1, D)),
            full((1, D)),
            full((1, D)),
            pl.BlockSpec((D, fc), lambda i, f: (0, f)),
            pl.BlockSpec((fc, D), lambda i, f: (f, 0)),
        ],
        out_specs=pl.BlockSpec((ns, T, D), tok),
        out_shape=jax.ShapeDtypeStruct((B, L, D), F32),
        scratch_shapes=[
            pltpu.VMEM((ns * T, D), BF16),
            pltpu.VMEM((ns * T, D), F32),
        ],
        compiler_params=pltpu.CompilerParams(
            dimension_semantics=("arbitrary", "arbitrary"), vmem_limit_bytes=VMEM_LIMIT),
        name="channel_mix",
    )(x, mix, g1, sh2, sc2, g2, wout_b, post_mix_w, pre_ffn_w, post_ffn_w, wup_b, wdn_b)


def _pick(n, pref):
    t = min(n, pref)
    while n % t:
        t //= 2
    return t


def _layer(x, mod, states, lw, ffn_ws, *, pos_base, mix_T, mix_ns, ffn_T, ffn_ns):
    (pre_mix_w, post_mix_w, pre_ffn_w, post_ffn_w, win_b, conv_w, conv_b, dtb, alog, dskip_e,
     normw, inv2) = lw
    B = x.shape[0]
    sh1, sc1, g1, sh2, sc2, g2 = [m.reshape(B, 1, D_MODEL) for m in jnp.split(mod, 6, axis=-1)]
    cast_ws = tuple(w for w in ffn_ws if w.dtype != BF16)
    res = _mix_call(
        x, sh1, sc1, pre_mix_w, win_b, inv2, conv_w, conv_b, dtb, alog, dskip_e, normw, states,
        cast_ws, T=mix_T, ns=mix_ns, pos_base=pos_base)
    mix, r_new, s_new, c_new = res[:4]
    if cast_ws:
        ffn_ws = tuple(res[4:])
    wout_b, wup_b, wdn_b = ffn_ws
    y = _ffn_call(x, mix, g1, sh2, sc2, g2, wout_b, post_mix_w, pre_ffn_w, post_ffn_w, wup_b, wdn_b,
                  T=ffn_T, ns=ffn_ns, fc=FFN_CHUNK)
    return y, r_new, s_new.reshape(B, SSM_HEADS, SSM_HEADDIM, SSM_STATE), c_new, ffn_ws


def kernel(x_prompt, x_sample, state_ret, state_ssm, state_conv, c_prompt, c_sample, w_ada, b_ada,
           pre_mix_w, post_mix_w, pre_ffn_w, post_ffn_w, w_in, conv_w, conv_b, dt_bias, a_log, d_skip,
           ssm_norm_w, w_out, w_up, w_down):
    depth = w_ada.shape[0]
    bp, lp, _ = x_prompt.shape
    bs, ls, _ = x_sample.shape
    half = RET_DIM // 2
    inv = ROPE_BASE ** (-jnp.arange(half, dtype=F32) / half)
    inv2 = jnp.concatenate([inv, inv]).reshape(1, RET_DIM)

    def pad_lanes(v):
        return jnp.pad(v.astype(F32), (0, LANES - v.shape[0])).reshape(1, LANES)

    yp, ys = x_prompt, x_sample
    outs = [[] for _ in range(6)]
    rows = bp + bs
    rows_pad = -(-rows // 8) * 8
    for l in range(depth):
        c_all = jnp.pad(jnp.concatenate([c_prompt, c_sample], axis=0), ((0, rows_pad - rows), (0, 0)))
        mod = _modulation(c_all, w_ada[l], b_ada[l])
        win_b = w_in[l].astype(BF16)
        lw = (pre_mix_w[l].reshape(1, -1), post_mix_w[l].reshape(1, -1), pre_ffn_w[l].reshape(1, -1),
              post_ffn_w[l].reshape(1, -1), win_b, conv_w[l], conv_b[l].reshape(1, -1),
              pad_lanes(dt_bias[l]), pad_lanes(a_log[l]),
              jnp.repeat(d_skip[l].astype(F32), SSM_HEADDIM).reshape(1, -1),
              ssm_norm_w[l].reshape(1, -1), inv2)
        mix_T = _pick(lp, 256)
        ffn_ws = (w_out[l], w_up[l], w_down[l])
        n_steps = bp * (lp // mix_T)
        if any(w.shape[0] % (16 * n_steps) for w in ffn_ws):
            ffn_ws = tuple(w.astype(BF16) for w in ffn_ws)
        yp, r, s, c, ffn_ws = _layer(yp, mod[:bp], None, lw, ffn_ws, pos_base=0,
                                     mix_T=mix_T, mix_ns=1, ffn_T=_pick(lp, 512), ffn_ns=1)
        outs[0].append(r); outs[1].append(s); outs[2].append(c)
        ns_mix = _pick(bs, max(1, 128 // ls))
        ns_ffn = _pick(bs, max(1, 512 // ls))
        st = (state_ret[l], state_ssm[l].reshape(bs, SSM_WIDTH, SSM_STATE), state_conv[l])
        ys, r, s, c, _ = _layer(ys, mod[bp:bp + bs], st, lw, ffn_ws, pos_base=PAST_LEN,
                                mix_T=ls, mix_ns=ns_mix, ffn_T=ls, ffn_ns=ns_ffn)
        outs[3].append(r); outs[4].append(s); outs[5].append(c)
    return (yp, ys) + tuple(jnp.stack(o) for o in outs)
```

```python
import functools
import math

import jax
import jax.numpy as jnp
from jax import lax
from jax.experimental import pallas as pl
from jax.experimental.pallas import tpu as pltpu

F32 = jnp.float32
BF16 = jnp.bfloat16

D_MODEL = 2048
PAST_LEN = 4096
CHUNK = 64
RET_HEADS = 8
RET_DIM = 128
RET_WIDTH = RET_HEADS * RET_DIM
SSM_HEADS = 16
SSM_HEADDIM = 64
SSM_WIDTH = SSM_HEADS * SSM_HEADDIM
SSM_GROUPS = 2
HEADS_PER_GROUP = SSM_HEADS // SSM_GROUPS
SSM_STATE = 128
CONV_WIDTH = 4
CONV_DIM = SSM_WIDTH + 2 * SSM_GROUPS * SSM_STATE
D_FF = 4 * D_MODEL
ROPE_BASE = 10000.0
EPS = 1e-6

LANES = 128
CONV_PAD = 8
OFF_Q = 0
OFF_K = RET_WIDTH
OFF_V = 2 * RET_WIDTH
OFF_G = 3 * RET_WIDTH
OFF_Z = 4 * RET_WIDTH
OFF_XBC = 4 * RET_WIDTH + SSM_WIDTH
OFF_DT = OFF_XBC + CONV_DIM
IN_COLS = OFF_DT + SSM_HEADS
PROJ_BLOCK = 512
FFN_CHUNK = 1024
FFN_ROWS = 16

VMEM_LIMIT = 60 * 1024 * 1024


def _silu(x):
    h = 0.5 * x
    return h + h * jnp.tanh(h)


def _softplus(x):
    return jnp.maximum(x, 0.0) + jnp.log1p(jnp.exp(-jnp.abs(x)))


def _split3(x):
    hi = x.astype(BF16)
    r1 = x - hi.astype(F32)
    mid = r1.astype(BF16)
    lo = (r1 - mid.astype(F32)).astype(BF16)
    return hi, mid, lo


def _mod_kernel(c_ref, w_ref, b_ref, o_ref):
    s = _silu(c_ref[...]).astype(BF16)
    o_ref[...] = jnp.dot(s, w_ref[...].astype(BF16), preferred_element_type=F32) + b_ref[...]


def _modulation(c, w_ada, b_ada):
    rows, d = c.shape
    n = w_ada.shape[1]
    bn = 1024
    return pl.pallas_call(
        _mod_kernel,
        grid=(n // bn,),
        in_specs=[
            pl.BlockSpec((rows, d), lambda j: (0, 0)),
            pl.BlockSpec((d, bn), lambda j: (0, j)),
            pl.BlockSpec((1, bn), lambda j: (0, j)),
        ],
        out_specs=pl.BlockSpec((rows, bn), lambda j: (0, j)),
        out_shape=jax.ShapeDtypeStruct((rows, n), F32),
        compiler_params=pltpu.CompilerParams(
            dimension_semantics=("arbitrary",), vmem_limit_bytes=VMEM_LIMIT),
        name="adaln_mod",
    )(c, w_ada, b_ada.reshape(1, n))


def _mix_kernel(*refs, T, ns, cl, pos_base, has_state, n_cast):
    it = iter(refs)
    x_ref, sh_ref, sc_ref, pw_ref, win_ref, inv_ref = (next(it) for _ in range(6))
    cw_ref, cb_ref, dtb_ref, alog_ref, dsk_ref, nw_ref = (next(it) for _ in range(6))
    if has_state:
        sret_ref, sssm_ref, sconv_ref = (next(it) for _ in range(3))
    cast_in = [next(it) for _ in range(n_cast)]
    mix_ref, ret_ref, ssm_ref, conv_ref = (next(it) for _ in range(4))
    cast_out = [next(it) for _ in range(n_cast)]
    (hb_s, q_s, k_s, kd_s, v_s, gz_s, ext_s, act_s, dt_s, ht_s,
     mask_s, qdec_s, kdec_s) = (next(it) for _ in range(13))

    b_idx = pl.program_id(0)
    t_idx = pl.program_id(1)
    R = ns * T
    lg = [math.log(1.0 - 2.0 ** (-5.0 - h)) for h in range(RET_HEADS)]

    for wi, wo in zip(cast_in, cast_out):
        wo[...] = wi[...].astype(BF16)

    @pl.when((b_idx == 0) & (t_idx == 0))
    def _tables():
        ii = lax.broadcasted_iota(jnp.int32, (T, T), 0)
        jj = lax.broadcasted_iota(jnp.int32, (T, T), 1)
        dist = jnp.abs(ii - jj).astype(F32)
        shift = int(math.log2(cl))
        visible = (jj >> shift) <= (ii >> shift)
        ri = lax.broadcasted_iota(jnp.int32, (T, LANES), 0).astype(F32)
        for h in range(RET_HEADS):
            mask_s[h] = jnp.where(visible, jnp.exp(lg[h] * dist), 0.0)
            qdec_s[h] = jnp.exp(lg[h] * (ri + 1.0))
            kdec_s[h] = jnp.exp(lg[h] * (T - 1.0 - ri))

    @pl.when(t_idx == 0)
    def _init():
        if has_state:
            ret_ref[...] = sret_ref[...]
            for s in range(ns):
                for pair in range(SSM_HEADS // 2):
                    pc = slice(pair * LANES, (pair + 1) * LANES)
                    ht_s[s, :, pc] = sssm_ref[s, pc, :].T
        else:
            ret_ref[...] = jnp.zeros(ret_ref.shape, F32)
            ht_s[...] = jnp.zeros(ht_s.shape, F32)
        ext_s[:, 0:CONV_PAD, :] = jnp.zeros((ns, CONV_PAD, CONV_DIM), F32)
        dt_s[...] = jnp.zeros(dt_s.shape, F32)
        if has_state:
            ext_s[:, CONV_PAD - (CONV_WIDTH - 1):CONV_PAD, :] = sconv_ref[...]

    pj = {}

    def proj_prologue():
        scale = pw_ref[...] * (1.0 + sc_ref[...])
        rc = min(T, 64)
        for r0 in range(0, T, rc):
            x3 = x_ref[:, r0:r0 + rc, :]
            ms = jnp.mean(x3 * x3, axis=-1, keepdims=True)
            hmod = x3 * lax.rsqrt(ms + EPS) * scale + sh_ref[...]
            for s in range(ns):
                hb_s[s * T + r0:s * T + r0 + rc, :] = hmod[s].astype(BF16)

    def rotary_tables():
        hr = R // 2
        row = lax.broadcasted_iota(jnp.int32, (hr, LANES), 0)
        low = lax.broadcasted_iota(jnp.int32, (hr, LANES), 1) < RET_DIM // 2
        row = jnp.where(low, row, row + hr)
        pos = (pos_base + t_idx * T + (row & (T - 1))).astype(F32)
        ang = pos * inv_ref[...]
        c2, s2 = jnp.cos(ang), jnp.sin(ang)
        c2r, s2r = pltpu.roll(c2, RET_DIM // 2, 1), pltpu.roll(s2, RET_DIM // 2, 1)
        pj["cos"] = jnp.concatenate([jnp.where(low, c2, c2r), jnp.where(low, c2r, c2)], axis=0)
        pj["sin"] = jnp.concatenate([jnp.where(low, -s2, s2r), jnp.where(low, -s2r, s2)], axis=0)

    def rot(u):
        return u * pj["cos"] + pltpu.roll(u, RET_DIM // 2, 1) * pj["sin"]

    def put(dst, c0, val):
        w = val.shape[1]
        for s in range(ns):
            dst[s, :, c0:c0 + w] = val[s * T:(s + 1) * T]

    def proj_block(c0):
        res = jnp.dot(hb_s[...], win_ref[:, c0:c0 + PROJ_BLOCK], preferred_element_type=F32)
        if c0 < OFF_V:
            for j in range(PROJ_BLOCK // RET_DIM):
                cj = c0 + j * RET_DIM
                r = rot(res[:, j * RET_DIM:(j + 1) * RET_DIM])
                if c0 < OFF_K:
                    put(q_s, cj, r.astype(BF16))
                else:
                    kf = r * (RET_DIM ** -0.5)
                    h = (cj - OFF_K) // RET_DIM
                    put(k_s, cj - OFF_K, kf.astype(BF16))
                    for s in range(ns):
                        kd_s[s, :, cj - OFF_K:cj - OFF_K + RET_DIM] = (
                            kf[s * T:(s + 1) * T] * kdec_s[h]).astype(BF16)
        elif c0 < OFF_G:
            put(v_s, c0 - OFF_V, res.astype(BF16))
        elif c0 < OFF_XBC:
            put(gz_s, c0 - OFF_G, _silu(res))
        else:
            for s in range(ns):
                ext_s[s, CONV_PAD:CONV_PAD + T, c0 - OFF_XBC:c0 - OFF_XBC + PROJ_BLOCK] = (
                    res[s * T:(s + 1) * T])

    def proj_dt():
        res = jnp.dot(hb_s[...], win_ref[:, OFF_DT:IN_COLS], preferred_element_type=F32)
        put(dt_s, 0, _softplus(res + dtb_ref[:, 0:SSM_HEADS]))

    proj_units = [proj_prologue]
    proj_units += [functools.partial(proj_block, c0) for c0 in range(0, OFF_DT, PROJ_BLOCK)]
    proj_units += [proj_dt]

    def mixer_units(s, slot=0):
        mx = {}
        units = []

        def consts():
            ti = lax.broadcasted_iota(jnp.int32, (T, T), 0)
            tj = lax.broadcasted_iota(jnp.int32, (T, T), 1)
            mx["causal"] = tj <= ti
            mx["tri"] = mx["causal"].astype(BF16)
            mx["low"] = lax.broadcasted_iota(jnp.int32, (T, LANES), 1) < SSM_HEADDIM
        units.append(consts)

        def ret_head(h):
            cs = slice(h * RET_DIM, (h + 1) * RET_DIM)
            qh = q_s[s, :, cs]
            kh = k_s[s, :, cs]
            kdh = kd_s[s, :, cs]
            vh = v_s[s, :, cs]
            S_old = ret_ref[s, h]
            sc = lax.dot_general(qh, kh, (((1,), (1,)), ((), ())), preferred_element_type=F32)
            p = (sc * mask_s[h]).astype(BF16)
            o = jnp.dot(p, vh, preferred_element_type=F32)
            o = o + jnp.dot(qh, S_old.astype(BF16), preferred_element_type=F32) * qdec_s[h]
            upd = lax.dot_general(kdh, vh, (((0,), (0,)), ((), ())), preferred_element_type=F32)
            ret_ref[s, h] = math.exp(lg[h] * T) * S_old + upd
            mu = jnp.mean(o, axis=-1, keepdims=True)
            d = o - mu
            var = jnp.mean(d * d, axis=-1, keepdims=True)
            mix_ref[s, :, cs] = (d * lax.rsqrt(var + EPS) * gz_s[s, :, cs]).astype(BF16)
        units += [functools.partial(ret_head, h) for h in range(RET_HEADS)]

        def conv_block(c0):
            cc = slice(c0, c0 + PROJ_BLOCK)
            base = CONV_PAD - (CONV_WIDTH - 1)
            acc = cb_ref[:, cc] + cw_ref[0:1, cc] * ext_s[s, base:base + T, cc]
            for j in range(1, CONV_WIDTH):
                acc = acc + cw_ref[j:j + 1, cc] * ext_s[s, base + j:base + j + T, cc]
            act_s[slot, :, cc] = _silu(acc)
        units += [functools.partial(conv_block, c0) for c0 in range(0, CONV_DIM, PROJ_BLOCK)]

        def conv_tail():
            conv_ref[s] = ext_s[s, CONV_PAD + T - (CONV_WIDTH - 1):CONV_PAD + T, :]
            ext_s[s, 0:CONV_PAD, :] = ext_s[s, T:T + CONV_PAD, :]

        def ssd_cum():
            conv_tail()
            dt = dt_s[s]
            a = dt * (-jnp.exp(alog_ref[...]))
            a_hi, a_mid, a_lo = _split3(a)
            tri = mx["tri"]
            cum = (jnp.dot(tri, a_hi, preferred_element_type=F32)
                   + jnp.dot(tri, a_mid, preferred_element_type=F32)
                   + jnp.dot(tri, a_lo, preferred_element_type=F32))
            mx["cum"] = cum
            mx["cum_t"] = cum.T
            mx["dt_t"] = dt.T
        units.append(ssd_cum)

        gcols = HEADS_PER_GROUP * SSM_HEADDIM

        def ssd_group(g):
            b_f = act_s[slot, :, SSM_WIDTH + g * SSM_STATE:SSM_WIDTH + (g + 1) * SSM_STATE]
            b_g = b_f.astype(BF16)
            c_g = act_s[slot, :, SSM_WIDTH + (SSM_GROUPS + g) * SSM_STATE:
                        SSM_WIDTH + (SSM_GROUPS + g + 1) * SSM_STATE].astype(BF16)
            mx["b_t"] = b_f.T
            mx["gmat"] = lax.dot_general(c_g, b_g, (((1,), (1,)), ((), ())),
                                         preferred_element_type=F32)
            h_old = ht_s[s, :, g * gcols:(g + 1) * gcols]
            mx["y_int"] = jnp.dot(c_g, h_old.astype(BF16), preferred_element_type=F32)

        def ssd_pair(g, m):
            causal, low = mx["causal"], mx["low"]
            cum, cum_t, dt_t = mx["cum"], mx["cum_t"], mx["dt_t"]
            pair = g * (HEADS_PER_GROUP // 2) + m
            h0 = 2 * pair
            pc = slice(pair * LANES, (pair + 1) * LANES)
            ws, bcs, es, decs = [], [], [], []
            for hh in (h0, h0 + 1):
                cbh = jnp.broadcast_to(cum[:, hh:hh + 1], (T, LANES))
                if T < LANES:
                    decay = (jnp.where(causal, jnp.exp(cbh[:, :T] - cum_t[hh:hh + 1, :]), 0.0)
                             * dt_t[hh:hh + 1, :])
                    ws.append((decay * mx["gmat"]).astype(BF16))
                    coef = decay[T - 1:T, :]
                else:
                    nb = T // LANES
                    diag = causal[0:LANES, 0:LANES]
                    wcols, coefs = [], []
                    for c in range(nb):
                        cc = slice(c * LANES, (c + 1) * LANES)
                        wrows = [jnp.zeros((c * LANES, LANES), BF16)] if c else []
                        for r in range(c, nb):
                            rr = slice(r * LANES, (r + 1) * LANES)
                            e = jnp.exp(cbh[rr] - cum_t[hh:hh + 1, cc])
                            d = (jnp.where(diag, e, 0.0) if r == c else e) * dt_t[hh:hh + 1, cc]
                            wrows.append((d * mx["gmat"][rr, cc]).astype(BF16))
                        wcols.append(jnp.concatenate(wrows, axis=0))
                        coefs.append(d[LANES - 1:LANES, :])
                    ws.append(jnp.concatenate(wcols, axis=1))
                    coef = jnp.concatenate(coefs, axis=1)
                bcs.append((mx["b_t"] * coef).astype(BF16))
                es.append(jnp.exp(cbh))
                decs.append(jnp.exp(cum_t[hh:hh + 1, T - 1:T]))
            xs_p = act_s[slot, :, pc]
            xs_b = xs_p.astype(BF16)
            zero = jnp.zeros_like(xs_b)
            rhs = jnp.concatenate([jnp.where(low, xs_b, zero),
                                   jnp.where(low, zero, xs_b)], axis=0)
            y = jnp.dot(jnp.concatenate(ws, axis=1), rhs, preferred_element_type=F32)
            y = (y + mx["y_int"][:, m * LANES:(m + 1) * LANES] * jnp.where(low, es[0], es[1])
                 + dsk_ref[:, pc] * xs_p)
            act_s[slot, :, pc] = y
            upd = jnp.dot(jnp.concatenate(bcs, axis=1), rhs, preferred_element_type=F32)
            low_n = lax.broadcasted_iota(jnp.int32, (SSM_STATE, LANES), 1) < SSM_HEADDIM
            ht_s[s, :, pc] = jnp.where(low_n, decs[0], decs[1]) * ht_s[s, :, pc] + upd

        def ssd_state(g):
            for m in range(HEADS_PER_GROUP // 2):
                pair = g * (HEADS_PER_GROUP // 2) + m
                pc = slice(pair * LANES, (pair + 1) * LANES)
                ssm_ref[s, pc, :] = ht_s[s, :, pc].T

        for g in range(SSM_GROUPS):
            units.append(functools.partial(ssd_group, g))
            units += [functools.partial(ssd_pair, g, m) for m in range(HEADS_PER_GROUP // 2)]
            units.append(functools.partial(ssd_state, g))

        def ssd_norm(g):
            gw = SSM_WIDTH // SSM_GROUPS
            gc = slice(g * gw, (g + 1) * gw)
            yz = act_s[slot, :, gc] * gz_s[s, :, RET_WIDTH + g * gw:RET_WIDTH + (g + 1) * gw]
            msq = jnp.mean(yz * yz, axis=-1, keepdims=True)
            mix_ref[s, :, RET_WIDTH + g * gw:RET_WIDTH + (g + 1) * gw] = (
                yz * lax.rsqrt(msq + EPS) * nw_ref[:, gc]).astype(BF16)
        units += [functools.partial(ssd_norm, g) for g in range(SSM_GROUPS)]
        return units

    if ns == 1:
        blk = {c0: u for c0, u in zip(range(0, OFF_DT, PROJ_BLOCK), proj_units[1:-1])}
        mu = mixer_units(0)
        proj_units[0]()
        mu[0]()
        for c0 in range(OFF_XBC, OFF_DT, PROJ_BLOCK):
            blk[c0]()
        proj_units[-1]()
        rotary_tables()
        order = list(range(OFF_Q, OFF_XBC, PROJ_BLOCK))
        after = [[9], [10], [11], [12, 13], [14, 15], [16, 17, 18], [19, 20, 21], [22, 23, 24],
                 [1, 2, 3, 4], [25, 26, 5, 6, 7, 8]]
        assert len(order) == len(after) and sorted(sum(after, [0])) == list(range(len(mu)))
        for c0, bis in zip(order, after):
            blk[c0]()
            for bi in bis:
                mu[bi]()
    else:
        rotary_tables()
        for au in proj_units:
            au()

        unroll = act_s.shape[0]

        def seq_body(i, carry):
            streams = [mixer_units(i * unroll + u, u) for u in range(unroll)]
            for step_units in zip(*streams):
                for bu in step_units:
                    bu()
            return carry

        lax.fori_loop(0, ns // unroll, seq_body, 0)


def _mix_call(x, sh1, sc1, pre_w, win_b, inv2, conv_w, conv_b, dtb, alog, dskip_e, normw,
              states, cast_ws, *, T, ns, pos_base):
    B, L, D = x.shape
    has_state = states is not None
    cl = min(L, CHUNK)
    n_t = L // T
    grid = (B // ns, n_t)
    n_steps = grid[0] * grid[1]
    R = ns * T
    unroll = math.gcd(ns, 4)

    def full(shape):
        nd = len(shape)
        return pl.BlockSpec(shape, lambda b, t, _nd=nd: (0,) * _nd, pipeline_mode=pl.Buffered(1))

    in_specs = [
        pl.BlockSpec((ns, T, D), lambda b, t: (b, t, 0)),
        pl.BlockSpec((ns, 1, D), lambda b, t: (b, 0, 0)),
        pl.BlockSpec((ns, 1, D), lambda b, t: (b, 0, 0)),
        full((1, D)),
        full(win_b.shape),
        full((1, LANES)),
        full(conv_w.shape),
        full((1, CONV_DIM)),
        full((1, LANES)),
        full((1, LANES)),
        full((1, SSM_WIDTH)),
        full((1, SSM_WIDTH)),
    ]
    args = [x, sh1, sc1, pre_w, win_b, inv2, conv_w, conv_b, dtb, alog, dskip_e, normw]
    if has_state:
        in_specs += [
            pl.BlockSpec((ns, RET_HEADS, RET_DIM, RET_DIM), lambda b, t: (b, 0, 0, 0)),
            pl.BlockSpec((ns, SSM_WIDTH, SSM_STATE), lambda b, t: (b, 0, 0)),
            pl.BlockSpec((ns, CONV_WIDTH - 1, CONV_DIM), lambda b, t: (b, 0, 0)),
        ]
        args += list(states)
    cast_specs = []
    for w in cast_ws:
        rows = w.shape[0] // n_steps
        assert rows * n_steps == w.shape[0] and rows % 16 == 0
        cast_specs.append(pl.BlockSpec((rows, w.shape[1]), lambda b, t: (b * n_t + t, 0)))
    in_specs += cast_specs
    args += list(cast_ws)
    out_specs = [
        pl.BlockSpec((ns, T, D), lambda b, t: (b, t, 0)),
        pl.BlockSpec((ns, RET_HEADS, RET_DIM, RET_DIM), lambda b, t: (b, 0, 0, 0)),
        pl.BlockSpec((ns, SSM_WIDTH, SSM_STATE), lambda b, t: (b, 0, 0)),
        pl.BlockSpec((ns, CONV_WIDTH - 1, CONV_DIM), lambda b, t: (b, 0, 0)),
    ]
    out_shape = [
        jax.ShapeDtypeStruct((B, L, D), BF16),
        jax.ShapeDtypeStruct((B, RET_HEADS, RET_DIM, RET_DIM), F32),
        jax.ShapeDtypeStruct((B, SSM_WIDTH, SSM_STATE), F32),
        jax.ShapeDtypeStruct((B, CONV_WIDTH - 1, CONV_DIM), F32),
    ]
    out_specs += cast_specs
    out_shape += [jax.ShapeDtypeStruct(w.shape, BF16) for w in cast_ws]
    scratch = [
        pltpu.VMEM((R, D), BF16),
        pltpu.VMEM((ns, T, RET_WIDTH), BF16),
        pltpu.VMEM((ns, T, RET_WIDTH), BF16),
        pltpu.VMEM((ns, T, RET_WIDTH), BF16),
        pltpu.VMEM((ns, T, RET_WIDTH), BF16),
        pltpu.VMEM((ns, T, 2 * RET_WIDTH), F32),
        pltpu.VMEM((ns, T + CONV_PAD, CONV_DIM), F32),
        pltpu.VMEM((unroll, T, CONV_DIM), F32),
        pltpu.VMEM((ns, T, LANES), F32),
        pltpu.VMEM((ns, SSM_STATE, SSM_WIDTH), F32),
        pltpu.VMEM((RET_HEADS, T, T), F32),
        pltpu.VMEM((RET_HEADS, T, LANES), F32),
        pltpu.VMEM((RET_HEADS, T, LANES), F32),
    ]
    kern = functools.partial(_mix_kernel, T=T, ns=ns, cl=cl, pos_base=pos_base, has_state=has_state,
                             n_cast=len(cast_ws))
    return pl.pallas_call(
        kern,
        grid=grid,
        in_specs=in_specs,
        out_specs=out_specs,
        out_shape=out_shape,
        scratch_shapes=scratch,
        compiler_params=pltpu.CompilerParams(
            dimension_semantics=("arbitrary", "arbitrary"), vmem_limit_bytes=VMEM_LIMIT),
        name="token_mix_state" if has_state else "token_mix",
    )(*args)


def _ffn_kernel(x_ref, mix0_ref, mixb_ref, mixn_ref, g1_ref, sh_ref, sc_ref, g2_ref, wout_ref, pmw_ref,
                pfw_ref, qfw_ref, wup_hbm, wdn_hbm, o_ref, h2_s, acc_s, wup_buf, wdn_buf, sem,
                *, ns, T, nf, fc):
    i = pl.program_id(0)
    n_tiles = pl.num_programs(0)
    R = ns * T

    def chunk_copies(f, slot):
        return (pltpu.make_async_copy(wup_hbm.at[:, pl.ds(f * fc, fc)], wup_buf.at[slot], sem.at[0, slot]),
                pltpu.make_async_copy(wdn_hbm.at[pl.ds(f * fc, fc), :], wdn_buf.at[slot], sem.at[1, slot]))

    def start(f, slot):
        for c in chunk_copies(f, slot):
            c.start()

    def wait(f, slot):
        for c in chunk_copies(f, slot):
            c.wait()
    rc = min(T, FFN_ROWS)
    halves = ((0, R // 2), (R // 2, R))

    def inv_rms(v):
        return lax.rsqrt(jnp.mean(v * v, axis=-1, keepdims=True) + EPS)

    def passes(lo, hi):
        return [(s, r0) for s in range(ns) for r0 in range(0, T, rc) if lo <= s * T + r0 < hi]

    def out_proj(half_ref, lo, hi):
        lhs = half_ref[0] if ns == 1 else half_ref[...].reshape(hi - lo, D_MODEL)
        acc_s[lo:hi, :] = jnp.dot(lhs, wout_ref[...], preferred_element_type=F32)

    def first_norms(lo, hi):
        for s, r0 in passes(lo, hi):
            rows = slice(s * T + r0, s * T + r0 + rc)
            m = acc_s[rows, :]
            x1 = x_ref[s, r0:r0 + rc, :] + m * inv_rms(m) * (pmw_ref[...] * g1_ref[s])
            o_ref[s, r0:r0 + rc, :] = x1
            h2_s[rows, :] = (x1 * inv_rms(x1) * (pfw_ref[...] * (1.0 + sc_ref[s])) + sh_ref[s]).astype(BF16)
            acc_s[rows, :] = jnp.zeros((rc, D_MODEL), F32)

    def mlp(lo, hi, slot):
        u = jnp.dot(h2_s[lo:hi, :], wup_buf[slot], preferred_element_type=F32)
        u = jnp.maximum(u, 0.0)
        u = (u * u).astype(BF16)
        acc_s[lo:hi, :] += jnp.dot(u, wdn_buf[slot], preferred_element_type=F32)

    def last_norms(lo, hi):
        for s, r0 in passes(lo, hi):
            a = acc_s[s * T + r0:s * T + r0 + rc, :]
            o_ref[s, r0:r0 + rc, :] = (o_ref[s, r0:r0 + rc, :]
                                       + a * inv_rms(a) * (qfw_ref[...] * g2_ref[s]))

    (a_lo, a_hi), (b_lo, b_hi) = halves

    @pl.when(i == 0)
    def _prime():
        start(0, 0)
        out_proj(mix0_ref, a_lo, a_hi)

    wait(0, 0)
    start(1, 1)
    out_proj(mixb_ref, b_lo, b_hi)
    for lo, hi in halves:
        first_norms(lo, hi)
        mlp(lo, hi, 0)

    def middle(f, carry):
        slot = f % 2
        wait(f, slot)
        start(f + 1, 1 - slot)
        mlp(0, R, slot)
        return carry

    lax.fori_loop(1, nf - 1, middle, 0)

    last_slot = (nf - 1) % 2
    wait(nf - 1, last_slot)

    @pl.when(i + 1 < n_tiles)
    def _next_tile():
        start(0, 0)

    for lo, hi in halves:
        mlp(lo, hi, last_slot)
    last_norms(a_lo, a_hi)
    out_proj(mixn_ref, a_lo, a_hi)
    last_norms(b_lo, b_hi)


def _ffn_call(x, mix, g1, sh2, sc2, g2, wout_b, post_mix_w, pre_ffn_w, post_ffn_w, wup_b, wdn_b,
              *, T, ns, fc):
    B, L, D = x.shape
    nt = L // T
    grid = (B // ns * nt,)
    nf = D_FF // fc

    def tok(i):
        return (i // nt, i % nt, 0)

    def seq(i):
        return (i // nt, 0, 0)

    if ns == 1:
        half_shape = (1, T // 2, D)

        def half(j, h):
            return (j // nt, 2 * (j % nt) + h, 0)
    else:
        assert nt == 1 and ns % 2 == 0
        half_shape = (ns // 2, T, D)

        def half(j, h):
            return (2 * j + h, 0, 0)

    def full(shape):
        return pl.BlockSpec(shape, lambda i: (0, 0), pipeline_mode=pl.Buffered(1))

    assert nf >= 2 and nf % 2 == 0 and (ns * T) % 32 == 0
    kern = functools.partial(_ffn_kernel, ns=ns, T=T, nf=nf, fc=fc)
    return pl.pallas_call(
        kern,
        grid=grid,
        in_specs=[
            pl.BlockSpec((ns, T, D), tok),
            pl.BlockSpec(half_shape, lambda i: half(0, 0), pipeline_mode=pl.Buffered(1)),
            pl.BlockSpec(half_shape, lambda i: half(i, 1)),
            pl.BlockSpec(half_shape, lambda i: half(jnp.minimum(i + 1, grid[0] - 1), 0)),
            pl.BlockSpec((ns, 1, D), seq),
            pl.BlockSpec((ns, 1, D), seq),
            pl.BlockSpec((ns, 1, D), seq),
            pl.BlockSpec((ns, 1, D), seq),
            full((D, D)),
            full((1, D)),
            full((1, D)),
            full((1, D)),
            pl.BlockSpec(memory_space=pl.ANY),
            pl.BlockSpec(memory_space=pl.ANY),
        ],
        out_specs=pl.BlockSpec((ns, T, D), tok),
        out_shape=jax.ShapeDtypeStruct((B, L, D), F32),
        scratch_shapes=[
            pltpu.VMEM((ns * T, D), BF16),
            pltpu.VMEM((ns * T, D), F32),
            pltpu.VMEM((2, D, fc), BF16),
            pltpu.VMEM((2, fc, D), BF16),
            pltpu.SemaphoreType.DMA((2, 2)),
        ],
        compiler_params=pltpu.CompilerParams(
            dimension_semantics=("arbitrary",), vmem_limit_bytes=VMEM_LIMIT),
        name="channel_mix",
    )(x, mix, mix, mix, g1, sh2, sc2, g2, wout_b, post_mix_w, pre_ffn_w, post_ffn_w, wup_b, wdn_b)


def _pick(n, pref):
    t = min(n, pref)
    while n % t:
        t //= 2
    return t


def _layer(x, mod, states, lw, ffn_ws, *, pos_base, mix_T, mix_ns, ffn_T, ffn_ns):
    (pre_mix_w, post_mix_w, pre_ffn_w, post_ffn_w, win_b, conv_w, conv_b, dtb, alog, dskip_e,
     normw, inv2) = lw
    B = x.shape[0]
    sh1, sc1, g1, sh2, sc2, g2 = [m.reshape(B, 1, D_MODEL) for m in jnp.split(mod, 6, axis=-1)]
    cast_ws = tuple(w for w in ffn_ws if w.dtype != BF16)
    res = _mix_call(
        x, sh1, sc1, pre_mix_w, win_b, inv2, conv_w, conv_b, dtb, alog, dskip_e, normw, states,
        cast_ws, T=mix_T, ns=mix_ns, pos_base=pos_base)
    mix, r_new, s_new, c_new = res[:4]
    if cast_ws:
        ffn_ws = tuple(res[4:])
    wout_b, wup_b, wdn_b = ffn_ws
    y = _ffn_call(x, mix, g1, sh2, sc2, g2, wout_b, post_mix_w, pre_ffn_w, post_ffn_w, wup_b, wdn_b,
                  T=ffn_T, ns=ffn_ns, fc=FFN_CHUNK)
    return y, r_new, s_new.reshape(B, SSM_HEADS, SSM_HEADDIM, SSM_STATE), c_new, ffn_ws


def kernel(x_prompt, x_sample, state_ret, state_ssm, state_conv, c_prompt, c_sample, w_ada, b_ada,
           pre_mix_w, post_mix_w, pre_ffn_w, post_ffn_w, w_in, conv_w, conv_b, dt_bias, a_log, d_skip,
           ssm_norm_w, w_out, w_up, w_down):
    depth = w_ada.shape[0]
    bp, lp, _ = x_prompt.shape
    bs, ls, _ = x_sample.shape
    half = RET_DIM // 2
    inv = ROPE_BASE ** (-jnp.arange(half, dtype=F32) / half)
    inv2 = jnp.concatenate([inv, inv]).reshape(1, RET_DIM)

    def pad_lanes(v):
        return jnp.pad(v.astype(F32), (0, LANES - v.shape[0])).reshape(1, LANES)

    yp, ys = x_prompt, x_sample
    outs = [[] for _ in range(6)]
    rows = bp + bs
    rows_pad = -(-rows // 8) * 8
    for l in range(depth):
        c_all = jnp.pad(jnp.concatenate([c_prompt, c_sample], axis=0), ((0, rows_pad - rows), (0, 0)))
        mod = _modulation(c_all, w_ada[l], b_ada[l])
        win_b = w_in[l].astype(BF16)
        lw = (pre_mix_w[l].reshape(1, -1), post_mix_w[l].reshape(1, -1), pre_ffn_w[l].reshape(1, -1),
              post_ffn_w[l].reshape(1, -1), win_b, conv_w[l], conv_b[l].reshape(1, -1),
              pad_lanes(dt_bias[l]), pad_lanes(a_log[l]),
              jnp.repeat(d_skip[l].astype(F32), SSM_HEADDIM).reshape(1, -1),
              ssm_norm_w[l].reshape(1, -1), inv2)
        mix_T = _pick(lp, 256)
        ffn_ws = (w_out[l], w_up[l], w_down[l])
        n_steps = bp * (lp // mix_T)
        if any(w.shape[0] % (16 * n_steps) for w in ffn_ws):
            ffn_ws = tuple(w.astype(BF16) for w in ffn_ws)
        yp, r, s, c, ffn_ws = _layer(yp, mod[:bp], None, lw, ffn_ws, pos_base=0,
                                     mix_T=mix_T, mix_ns=1, ffn_T=_pick(lp, 512), ffn_ns=1)
        outs[0].append(r); outs[1].append(s); outs[2].append(c)
        ns_mix = _pick(bs, max(1, 128 // ls))
        ns_ffn = _pick(bs, max(1, 512 // ls))
        st = (state_ret[l], state_ssm[l].reshape(bs, SSM_WIDTH, SSM_STATE), state_conv[l])
        ys, r, s, c, _ = _layer(ys, mod[bp:bp + bs], st, lw, ffn_ws, pos_base=PAST_LEN,
                                mix_T=ls, mix_ns=ns_mix, ffn_T=ls, ffn_ns=ns_ffn)
        outs[3].append(r); outs[4].append(s); outs[5].append(c)
    return (yp, ys) + tuple(jnp.stack(o) for o in outs)
```

```python
import functools
import math

import jax
import jax.numpy as jnp
from jax import lax
from jax.experimental import pallas as pl
from jax.experimental.pallas import tpu as pltpu

F32 = jnp.float32
BF16 = jnp.bfloat16

D_MODEL = 2048
PAST_LEN = 4096
CHUNK = 64
RET_HEADS = 8
RET_DIM = 128
RET_WIDTH = RET_HEADS * RET_DIM
SSM_HEADS = 16
SSM_HEADDIM = 64
SSM_WIDTH = SSM_HEADS * SSM_HEADDIM
SSM_GROUPS = 2
HEADS_PER_GROUP = SSM_HEADS // SSM_GROUPS
SSM_STATE = 128
CONV_WIDTH = 4
CONV_DIM = SSM_WIDTH + 2 * SSM_GROUPS * SSM_STATE
D_FF = 4 * D_MODEL
ROPE_BASE = 10000.0
EPS = 1e-6

LANES = 128
CONV_PAD = 8
OFF_Q = 0
OFF_K = RET_WIDTH
OFF_V = 2 * RET_WIDTH
OFF_G = 3 * RET_WIDTH
OFF_Z = 4 * RET_WIDTH
OFF_XBC = 4 * RET_WIDTH + SSM_WIDTH
OFF_DT = OFF_XBC + CONV_DIM
IN_COLS = OFF_DT + SSM_HEADS
PROJ_BLOCK = 512
FFN_CHUNK = 1024
FFN_ROWS = 16

VMEM_LIMIT = 60 * 1024 * 1024


def _silu(x):
    h = 0.5 * x
    return h + h * jnp.tanh(h)


def _softplus(x):
    return jnp.maximum(x, 0.0) + jnp.log1p(jnp.exp(-jnp.abs(x)))


def _split3(x):
    hi = x.astype(BF16)
    r1 = x - hi.astype(F32)
    mid = r1.astype(BF16)
    lo = (r1 - mid.astype(F32)).astype(BF16)
    return hi, mid, lo


def _mod_kernel(c_ref, w_ref, b_ref, o_ref):
    s = _silu(c_ref[...]).astype(BF16)
    o_ref[...] = jnp.dot(s, w_ref[...].astype(BF16), preferred_element_type=F32) + b_ref[...]


def _modulation(c, w_ada, b_ada):
    rows, d = c.shape
    n = w_ada.shape[1]
    bn = 1024
    return pl.pallas_call(
        _mod_kernel,
        grid=(n // bn,),
        in_specs=[
            pl.BlockSpec((rows, d), lambda j: (0, 0)),
            pl.BlockSpec((d, bn), lambda j: (0, j)),
            pl.BlockSpec((1, bn), lambda j: (0, j)),
        ],
        out_specs=pl.BlockSpec((rows, bn), lambda j: (0, j)),
        out_shape=jax.ShapeDtypeStruct((rows, n), F32),
        compiler_params=pltpu.CompilerParams(
            dimension_semantics=("arbitrary",), vmem_limit_bytes=VMEM_LIMIT),
        name="adaln_mod",
    )(c, w_ada, b_ada.reshape(1, n))


def _mix_kernel(*refs, T, ns, cl, pos_base, has_state, n_cast):
    it = iter(refs)
    x_ref, sh_ref, sc_ref, pw_ref, win_ref, inv_ref = (next(it) for _ in range(6))
    cw_ref, cb_ref, dtb_ref, alog_ref, dsk_ref, nw_ref = (next(it) for _ in range(6))
    if has_state:
        sret_ref, sssm_ref, sconv_ref = (next(it) for _ in range(3))
    cast_in = [next(it) for _ in range(n_cast)]
    mix_ref, ret_ref, ssm_ref, conv_ref = (next(it) for _ in range(4))
    cast_out = [next(it) for _ in range(n_cast)]
    (hb_s, q_s, k_s, kd_s, v_s, gz_s, ext_s, act_s, dt_s, ht_s,
     mask_s, qdec_s, kdec_s) = (next(it) for _ in range(13))

    b_idx = pl.program_id(0)
    t_idx = pl.program_id(1)
    R = ns * T
    lg = [math.log(1.0 - 2.0 ** (-5.0 - h)) for h in range(RET_HEADS)]

    for wi, wo in zip(cast_in, cast_out):
        if len(wo.shape) == 2:
            wo[...] = wi[...].astype(BF16)
        else:
            cw = wo.shape[2]
            for f in range(wo.shape[0]):
                wo[f] = wi[:, f * cw:(f + 1) * cw].astype(BF16)

    @pl.when((b_idx == 0) & (t_idx == 0))
    def _tables():
        ii = lax.broadcasted_iota(jnp.int32, (T, T), 0)
        jj = lax.broadcasted_iota(jnp.int32, (T, T), 1)
        dist = jnp.abs(ii - jj).astype(F32)
        shift = int(math.log2(cl))
        visible = (jj >> shift) <= (ii >> shift)
        ri = lax.broadcasted_iota(jnp.int32, (T, LANES), 0).astype(F32)
        for h in range(RET_HEADS):
            mask_s[h] = jnp.where(visible, jnp.exp(lg[h] * dist), 0.0)
            qdec_s[h] = jnp.exp(lg[h] * (ri + 1.0))
            kdec_s[h] = jnp.exp(lg[h] * (T - 1.0 - ri))

    @pl.when(t_idx == 0)
    def _init():
        if has_state:
            ret_ref[...] = sret_ref[...]
            for s in range(ns):
                for pair in range(SSM_HEADS // 2):
                    pc = slice(pair * LANES, (pair + 1) * LANES)
                    ht_s[s, :, pc] = sssm_ref[s, pc, :].T
        else:
            ret_ref[...] = jnp.zeros(ret_ref.shape, F32)
            ht_s[...] = jnp.zeros(ht_s.shape, F32)
        ext_s[:, 0:CONV_PAD, :] = jnp.zeros((ns, CONV_PAD, CONV_DIM), F32)
        dt_s[...] = jnp.zeros(dt_s.shape, F32)
        if has_state:
            ext_s[:, CONV_PAD - (CONV_WIDTH - 1):CONV_PAD, :] = sconv_ref[...]

    pj = {}

    def proj_prologue():
        scale = pw_ref[...] * (1.0 + sc_ref[...])
        rc = min(T, 64)
        for r0 in range(0, T, rc):
            x3 = x_ref[:, r0:r0 + rc, :]
            ms = jnp.mean(x3 * x3, axis=-1, keepdims=True)
            hmod = x3 * lax.rsqrt(ms + EPS) * scale + sh_ref[...]
            for s in range(ns):
                hb_s[s * T + r0:s * T + r0 + rc, :] = hmod[s].astype(BF16)

    def rotary_tables():
        hr = R // 2
        row = lax.broadcasted_iota(jnp.int32, (hr, LANES), 0)
        low = lax.broadcasted_iota(jnp.int32, (hr, LANES), 1) < RET_DIM // 2
        row = jnp.where(low, row, row + hr)
        pos = (pos_base + t_idx * T + (row & (T - 1))).astype(F32)
        ang = pos * inv_ref[...]
        c2, s2 = jnp.cos(ang), jnp.sin(ang)
        c2r, s2r = pltpu.roll(c2, RET_DIM // 2, 1), pltpu.roll(s2, RET_DIM // 2, 1)
        pj["cos"] = jnp.concatenate([jnp.where(low, c2, c2r), jnp.where(low, c2r, c2)], axis=0)
        pj["sin"] = jnp.concatenate([jnp.where(low, -s2, s2r), jnp.where(low, -s2r, s2)], axis=0)

    def rot(u):
        return u * pj["cos"] + pltpu.roll(u, RET_DIM // 2, 1) * pj["sin"]

    def put(dst, c0, val):
        w = val.shape[1]
        for s in range(ns):
            dst[s, :, c0:c0 + w] = val[s * T:(s + 1) * T]

    def proj_block(c0):
        res = jnp.dot(hb_s[...], win_ref[:, c0:c0 + PROJ_BLOCK], preferred_element_type=F32)
        if c0 < OFF_V:
            for j in range(PROJ_BLOCK // RET_DIM):
                cj = c0 + j * RET_DIM
                r = rot(res[:, j * RET_DIM:(j + 1) * RET_DIM])
                if c0 < OFF_K:
                    put(q_s, cj, r.astype(BF16))
                else:
                    kf = r * (RET_DIM ** -0.5)
                    h = (cj - OFF_K) // RET_DIM
                    put(k_s, cj - OFF_K, kf.astype(BF16))
                    for s in range(ns):
                        kd_s[s, :, cj - OFF_K:cj - OFF_K + RET_DIM] = (
                            kf[s * T:(s + 1) * T] * kdec_s[h]).astype(BF16)
        elif c0 < OFF_G:
            put(v_s, c0 - OFF_V, res.astype(BF16))
        elif c0 < OFF_XBC:
            put(gz_s, c0 - OFF_G, _silu(res))
        else:
            for s in range(ns):
                ext_s[s, CONV_PAD:CONV_PAD + T, c0 - OFF_XBC:c0 - OFF_XBC + PROJ_BLOCK] = (
                    res[s * T:(s + 1) * T])

    def proj_dt():
        res = jnp.dot(hb_s[...], win_ref[:, OFF_DT:IN_COLS], preferred_element_type=F32)
        put(dt_s, 0, _softplus(res + dtb_ref[:, 0:SSM_HEADS]))

    proj_units = [proj_prologue]
    proj_units += [functools.partial(proj_block, c0) for c0 in range(0, OFF_DT, PROJ_BLOCK)]
    proj_units += [proj_dt]

    def mixer_units(s, slot=0):
        mx = {}
        units = []

        def consts():
            ti = lax.broadcasted_iota(jnp.int32, (T, T), 0)
            tj = lax.broadcasted_iota(jnp.int32, (T, T), 1)
            mx["causal"] = tj <= ti
            mx["tri"] = mx["causal"].astype(BF16)
            mx["low"] = lax.broadcasted_iota(jnp.int32, (T, LANES), 1) < SSM_HEADDIM
        units.append(consts)

        def ret_head(h):
            cs = slice(h * RET_DIM, (h + 1) * RET_DIM)
            qh = q_s[s, :, cs]
            kh = k_s[s, :, cs]
            kdh = kd_s[s, :, cs]
            vh = v_s[s, :, cs]
            S_old = ret_ref[s, h]
            sc = lax.dot_general(qh, kh, (((1,), (1,)), ((), ())), preferred_element_type=F32)
            p = (sc * mask_s[h]).astype(BF16)
            o = jnp.dot(p, vh, preferred_element_type=F32)
            o = o + jnp.dot(qh, S_old.astype(BF16), preferred_element_type=F32) * qdec_s[h]
            upd = lax.dot_general(kdh, vh, (((0,), (0,)), ((), ())), preferred_element_type=F32)
            ret_ref[s, h] = math.exp(lg[h] * T) * S_old + upd
            mu = jnp.mean(o, axis=-1, keepdims=True)
            d = o - mu
            var = jnp.mean(d * d, axis=-1, keepdims=True)
            mix_ref[s, :, cs] = (d * lax.rsqrt(var + EPS) * gz_s[s, :, cs]).astype(BF16)
        units += [functools.partial(ret_head, h) for h in range(RET_HEADS)]

        def conv_block(c0):
            cc = slice(c0, c0 + PROJ_BLOCK)
            base = CONV_PAD - (CONV_WIDTH - 1)
            acc = cb_ref[:, cc] + cw_ref[0:1, cc] * ext_s[s, base:base + T, cc]
            for j in range(1, CONV_WIDTH):
                acc = acc + cw_ref[j:j + 1, cc] * ext_s[s, base + j:base + j + T, cc]
            act_s[slot, :, cc] = _silu(acc)
        units += [functools.partial(conv_block, c0) for c0 in range(0, CONV_DIM, PROJ_BLOCK)]

        def conv_tail():
            conv_ref[s] = ext_s[s, CONV_PAD + T - (CONV_WIDTH - 1):CONV_PAD + T, :]
            ext_s[s, 0:CONV_PAD, :] = ext_s[s, T:T + CONV_PAD, :]

        def ssd_cum():
            conv_tail()
            dt = dt_s[s]
            a = dt * (-jnp.exp(alog_ref[...]))
            a_hi, a_mid, a_lo = _split3(a)
            tri = mx["tri"]
            cum = (jnp.dot(tri, a_hi, preferred_element_type=F32)
                   + jnp.dot(tri, a_mid, preferred_element_type=F32)
                   + jnp.dot(tri, a_lo, preferred_element_type=F32))
            mx["cum"] = cum
            mx["cum_t"] = cum.T
            mx["dt_t"] = dt.T
        units.append(ssd_cum)

        gcols = HEADS_PER_GROUP * SSM_HEADDIM

        def ssd_group(g):
            b_f = act_s[slot, :, SSM_WIDTH + g * SSM_STATE:SSM_WIDTH + (g + 1) * SSM_STATE]
            b_g = b_f.astype(BF16)
            c_g = act_s[slot, :, SSM_WIDTH + (SSM_GROUPS + g) * SSM_STATE:
                        SSM_WIDTH + (SSM_GROUPS + g + 1) * SSM_STATE].astype(BF16)
            mx["b_t"] = b_f.T
            mx["gmat"] = lax.dot_general(c_g, b_g, (((1,), (1,)), ((), ())),
                                         preferred_element_type=F32)
            h_old = ht_s[s, :, g * gcols:(g + 1) * gcols]
            mx["y_int"] = jnp.dot(c_g, h_old.astype(BF16), preferred_element_type=F32)

        def ssd_pair(g, m):
            causal, low = mx["causal"], mx["low"]
            cum, cum_t, dt_t = mx["cum"], mx["cum_t"], mx["dt_t"]
            pair = g * (HEADS_PER_GROUP // 2) + m
            h0 = 2 * pair
            pc = slice(pair * LANES, (pair + 1) * LANES)
            ws, bcs, es, decs = [], [], [], []
            for hh in (h0, h0 + 1):
                cbh = jnp.broadcast_to(cum[:, hh:hh + 1], (T, LANES))
                if T < LANES:
                    decay = (jnp.where(causal, jnp.exp(cbh[:, :T] - cum_t[hh:hh + 1, :]), 0.0)
                             * dt_t[hh:hh + 1, :])
                    ws.append((decay * mx["gmat"]).astype(BF16))
                    coef = decay[T - 1:T, :]
                else:
                    nb = T // LANES
                    diag = causal[0:LANES, 0:LANES]
                    wcols, coefs = [], []
                    for c in range(nb):
                        cc = slice(c * LANES, (c + 1) * LANES)
                        wrows = [jnp.zeros((c * LANES, LANES), BF16)] if c else []
                        for r in range(c, nb):
                            rr = slice(r * LANES, (r + 1) * LANES)
                            e = jnp.exp(cbh[rr] - cum_t[hh:hh + 1, cc])
                            d = (jnp.where(diag, e, 0.0) if r == c else e) * dt_t[hh:hh + 1, cc]
                            wrows.append((d * mx["gmat"][rr, cc]).astype(BF16))
                        wcols.append(jnp.concatenate(wrows, axis=0))
                        coefs.append(d[LANES - 1:LANES, :])
                    ws.append(jnp.concatenate(wcols, axis=1))
                    coef = jnp.concatenate(coefs, axis=1)
                bcs.append((mx["b_t"] * coef).astype(BF16))
                es.append(jnp.exp(cbh))
                decs.append(jnp.exp(cum_t[hh:hh + 1, T - 1:T]))
            xs_p = act_s[slot, :, pc]
            xs_b = xs_p.astype(BF16)
            zero = jnp.zeros_like(xs_b)
            rhs = jnp.concatenate([jnp.where(low, xs_b, zero),
                                   jnp.where(low, zero, xs_b)], axis=0)
            y = jnp.dot(jnp.concatenate(ws, axis=1), rhs, preferred_element_type=F32)
            y = (y + mx["y_int"][:, m * LANES:(m + 1) * LANES] * jnp.where(low, es[0], es[1])
                 + dsk_ref[:, pc] * xs_p)
            act_s[slot, :, pc] = y
            upd = jnp.dot(jnp.concatenate(bcs, axis=1), rhs, preferred_element_type=F32)
            low_n = lax.broadcasted_iota(jnp.int32, (SSM_STATE, LANES), 1) < SSM_HEADDIM
            ht_s[s, :, pc] = jnp.where(low_n, decs[0], decs[1]) * ht_s[s, :, pc] + upd

        def ssd_state(g):
            for m in range(HEADS_PER_GROUP // 2):
                pair = g * (HEADS_PER_GROUP // 2) + m
                pc = slice(pair * LANES, (pair + 1) * LANES)
                ssm_ref[s, pc, :] = ht_s[s, :, pc].T

        for g in range(SSM_GROUPS):
            units.append(functools.partial(ssd_group, g))
            units += [functools.partial(ssd_pair, g, m) for m in range(HEADS_PER_GROUP // 2)]
            units.append(functools.partial(ssd_state, g))

        def ssd_norm(g):
            gw = SSM_WIDTH // SSM_GROUPS
            gc = slice(g * gw, (g + 1) * gw)
            yz = act_s[slot, :, gc] * gz_s[s, :, RET_WIDTH + g * gw:RET_WIDTH + (g + 1) * gw]
            msq = jnp.mean(yz * yz, axis=-1, keepdims=True)
            mix_ref[s, :, RET_WIDTH + g * gw:RET_WIDTH + (g + 1) * gw] = (
                yz * lax.rsqrt(msq + EPS) * nw_ref[:, gc]).astype(BF16)
        units += [functools.partial(ssd_norm, g) for g in range(SSM_GROUPS)]
        return units

    if ns == 1:
        blk = {c0: u for c0, u in zip(range(0, OFF_DT, PROJ_BLOCK), proj_units[1:-1])}
        mu = mixer_units(0)
        proj_units[0]()
        mu[0]()
        for c0 in range(OFF_XBC, OFF_DT, PROJ_BLOCK):
            blk[c0]()
        proj_units[-1]()
        rotary_tables()
        order = list(range(OFF_Q, OFF_XBC, PROJ_BLOCK))
        after = [[9], [10], [11], [12, 13], [14, 15], [16, 17, 18], [19, 20, 21], [22, 23, 24],
                 [1, 2, 3, 4], [25, 26, 5, 6, 7, 8]]
        assert len(order) == len(after) and sorted(sum(after, [0])) == list(range(len(mu)))
        for c0, bis in zip(order, after):
            blk[c0]()
            for bi in bis:
                mu[bi]()
    else:
        rotary_tables()
        for au in proj_units:
            au()

        unroll = act_s.shape[0]

        def seq_body(i, carry):
            streams = [mixer_units(i * unroll + u, u) for u in range(unroll)]
            for step_units in zip(*streams):
                for bu in step_units:
                    bu()
            return carry

        lax.fori_loop(0, ns // unroll, seq_body, 0)


def _mix_call(x, sh1, sc1, pre_w, win_b, inv2, conv_w, conv_b, dtb, alog, dskip_e, normw,
              states, cast_ws, *, T, ns, pos_base):
    B, L, D = x.shape
    has_state = states is not None
    cl = min(L, CHUNK)
    n_t = L // T
    grid = (B // ns, n_t)
    n_steps = grid[0] * grid[1]
    R = ns * T
    unroll = math.gcd(ns, 4)

    def full(shape):
        nd = len(shape)
        return pl.BlockSpec(shape, lambda b, t, _nd=nd: (0,) * _nd, pipeline_mode=pl.Buffered(1))

    in_specs = [
        pl.BlockSpec((ns, T, D), lambda b, t: (b, t, 0)),
        pl.BlockSpec((ns, 1, D), lambda b, t: (b, 0, 0)),
        pl.BlockSpec((ns, 1, D), lambda b, t: (b, 0, 0)),
        full((1, D)),
        full(win_b.shape),
        full((1, LANES)),
        full(conv_w.shape),
        full((1, CONV_DIM)),
        full((1, LANES)),
        full((1, LANES)),
        full((1, SSM_WIDTH)),
        full((1, SSM_WIDTH)),
    ]
    args = [x, sh1, sc1, pre_w, win_b, inv2, conv_w, conv_b, dtb, alog, dskip_e, normw]
    if has_state:
        in_specs += [
            pl.BlockSpec((ns, RET_HEADS, RET_DIM, RET_DIM), lambda b, t: (b, 0, 0, 0)),
            pl.BlockSpec((ns, SSM_WIDTH, SSM_STATE), lambda b, t: (b, 0, 0)),
            pl.BlockSpec((ns, CONV_WIDTH - 1, CONV_DIM), lambda b, t: (b, 0, 0)),
        ]
        args += list(states)
    cast_out_specs, cast_out_shapes = [], []
    for w, nch in cast_ws:
        rows = w.shape[0] // n_steps
        assert rows * n_steps == w.shape[0] and rows % 16 == 0 and w.shape[1] % nch == 0
        in_specs.append(pl.BlockSpec((rows, w.shape[1]), lambda b, t: (b * n_t + t, 0)))
        args.append(w)
        if nch == 1:
            cast_out_specs.append(pl.BlockSpec((rows, w.shape[1]), lambda b, t: (b * n_t + t, 0)))
            cast_out_shapes.append(jax.ShapeDtypeStruct(w.shape, BF16))
        else:
            cw = w.shape[1] // nch
            cast_out_specs.append(pl.BlockSpec((nch, rows, cw), lambda b, t: (0, b * n_t + t, 0)))
            cast_out_shapes.append(jax.ShapeDtypeStruct((nch, w.shape[0], cw), BF16))
    out_specs = [
        pl.BlockSpec((ns, T, D), lambda b, t: (b, t, 0)),
        pl.BlockSpec((ns, RET_HEADS, RET_DIM, RET_DIM), lambda b, t: (b, 0, 0, 0)),
        pl.BlockSpec((ns, SSM_WIDTH, SSM_STATE), lambda b, t: (b, 0, 0)),
        pl.BlockSpec((ns, CONV_WIDTH - 1, CONV_DIM), lambda b, t: (b, 0, 0)),
    ]
    out_shape = [
        jax.ShapeDtypeStruct((B, L, D), BF16),
        jax.ShapeDtypeStruct((B, RET_HEADS, RET_DIM, RET_DIM), F32),
        jax.ShapeDtypeStruct((B, SSM_WIDTH, SSM_STATE), F32),
        jax.ShapeDtypeStruct((B, CONV_WIDTH - 1, CONV_DIM), F32),
    ]
    out_specs += cast_out_specs
    out_shape += cast_out_shapes
    scratch = [
        pltpu.VMEM((R, D), BF16),
        pltpu.VMEM((ns, T, RET_WIDTH), BF16),
        pltpu.VMEM((ns, T, RET_WIDTH), BF16),
        pltpu.VMEM((ns, T, RET_WIDTH), BF16),
        pltpu.VMEM((ns, T, RET_WIDTH), BF16),
        pltpu.VMEM((ns, T, 2 * RET_WIDTH), F32),
        pltpu.VMEM((ns, T + CONV_PAD, CONV_DIM), F32),
        pltpu.VMEM((unroll, T, CONV_DIM), F32),
        pltpu.VMEM((ns, T, LANES), F32),
        pltpu.VMEM((ns, SSM_STATE, SSM_WIDTH), F32),
        pltpu.VMEM((RET_HEADS, T, T), F32),
        pltpu.VMEM((RET_HEADS, T, LANES), F32),
        pltpu.VMEM((RET_HEADS, T, LANES), F32),
    ]
    kern = functools.partial(_mix_kernel, T=T, ns=ns, cl=cl, pos_base=pos_base, has_state=has_state,
                             n_cast=len(cast_ws))
    return pl.pallas_call(
        kern,
        grid=grid,
        in_specs=in_specs,
        out_specs=out_specs,
        out_shape=out_shape,
        scratch_shapes=scratch,
        compiler_params=pltpu.CompilerParams(
            dimension_semantics=("arbitrary", "arbitrary"), vmem_limit_bytes=VMEM_LIMIT),
        name="token_mix_state" if has_state else "token_mix",
    )(*args)


def _ffn_kernel(x_ref, mix_ref, g1_ref, sh_ref, sc_ref, g2_ref, wout_ref, pmw_ref, pfw_ref, qfw_ref,
                wup_hbm, wdn_hbm, o_ref, h2_s, acc_s, wup_buf, wdn_buf, sem, *, ns, T, nf, fc):
    i = pl.program_id(0)
    n_tiles = pl.num_programs(0)
    R = ns * T

    def chunk_copies(f, slot):
        return (pltpu.make_async_copy(wup_hbm.at[f], wup_buf.at[slot], sem.at[0, slot]),
                pltpu.make_async_copy(wdn_hbm.at[pl.ds(f * fc, fc), :], wdn_buf.at[slot], sem.at[1, slot]))

    def start(f, slot):
        for c in chunk_copies(f, slot):
            c.start()

    def wait(f, slot):
        for c in chunk_copies(f, slot):
            c.wait()
    rc = min(T, FFN_ROWS)
    halves = ((0, R // 2), (R // 2, R))

    def inv_rms(v):
        return lax.rsqrt(jnp.mean(v * v, axis=-1, keepdims=True) + EPS)

    def passes(lo, hi):
        return [(s, r0) for s in range(ns) for r0 in range(0, T, rc) if lo <= s * T + r0 < hi]

    def out_proj(lo, hi):
        lhs = mix_ref[0, lo:hi, :] if ns == 1 else mix_ref[lo // T:hi // T].reshape(hi - lo, D_MODEL)
        acc_s[lo:hi, :] = jnp.dot(lhs, wout_ref[...], preferred_element_type=F32)

    def first_norms(lo, hi):
        for s, r0 in passes(lo, hi):
            rows = slice(s * T + r0, s * T + r0 + rc)
            m = acc_s[rows, :]
            x1 = x_ref[s, r0:r0 + rc, :] + m * inv_rms(m) * (pmw_ref[...] * g1_ref[s])
            o_ref[s, r0:r0 + rc, :] = x1
            h2_s[rows, :] = (x1 * inv_rms(x1) * (pfw_ref[...] * (1.0 + sc_ref[s])) + sh_ref[s]).astype(BF16)
            acc_s[rows, :] = jnp.zeros((rc, D_MODEL), F32)

    def mlp(lo, hi, slot):
        u = jnp.dot(h2_s[lo:hi, :], wup_buf[slot], preferred_element_type=F32)
        u = jnp.maximum(u, 0.0)
        u = (u * u).astype(BF16)
        acc_s[lo:hi, :] += jnp.dot(u, wdn_buf[slot], preferred_element_type=F32)

    def last_norms(lo, hi):
        for s, r0 in passes(lo, hi):
            a = acc_s[s * T + r0:s * T + r0 + rc, :]
            o_ref[s, r0:r0 + rc, :] = (o_ref[s, r0:r0 + rc, :]
                                       + a * inv_rms(a) * (qfw_ref[...] * g2_ref[s]))

    @pl.when(i == 0)
    def _prime():
        start(0, 0)

    wait(0, 0)
    start(1, 1)
    for lo, hi in halves:
        out_proj(lo, hi)
    for lo, hi in halves:
        first_norms(lo, hi)
        mlp(lo, hi, 0)

    def middle(f, carry):
        slot = f % 2
        wait(f, slot)
        start(f + 1, 1 - slot)
        mlp(0, R, slot)
        return carry

    lax.fori_loop(1, nf - 1, middle, 0)

    last_slot = (nf - 1) % 2
    wait(nf - 1, last_slot)

    @pl.when(i + 1 < n_tiles)
    def _next_tile():
        start(0, 0)

    for lo, hi in halves:
        mlp(lo, hi, last_slot)
    for lo, hi in halves:
        last_norms(lo, hi)


def _ffn_call(x, mix, g1, sh2, sc2, g2, wout_b, post_mix_w, pre_ffn_w, post_ffn_w, wup_b, wdn_b,
              *, T, ns, fc):
    B, L, D = x.shape
    nt = L // T
    grid = (B // ns * nt,)
    nf = D_FF // fc

    def tok(i):
        return (i // nt, i % nt, 0)

    def seq(i):
        return (i // nt, 0, 0)

    def full(shape):
        return pl.BlockSpec(shape, lambda i: (0, 0), pipeline_mode=pl.Buffered(1))

    assert nf >= 2 and nf % 2 == 0 and (ns * T) % 32 == 0 and wup_b.shape == (nf, D, fc)
    kern = functools.partial(_ffn_kernel, ns=ns, T=T, nf=nf, fc=fc)
    return pl.pallas_call(
        kern,
        grid=grid,
        in_specs=[
            pl.BlockSpec((ns, T, D), tok),
            pl.BlockSpec((ns, T, D), tok),
            pl.BlockSpec((ns, 1, D), seq),
            pl.BlockSpec((ns, 1, D), seq),
            pl.BlockSpec((ns, 1, D), seq),
            pl.BlockSpec((ns, 1, D), seq),
            full((D, D)),
            full((1, D)),
            full((1, D)),
            full((1, D)),
            pl.BlockSpec(memory_space=pl.ANY),
            pl.BlockSpec(memory_space=pl.ANY),
        ],
        out_specs=pl.BlockSpec((ns, T, D), tok),
        out_shape=jax.ShapeDtypeStruct((B, L, D), F32),
        scratch_shapes=[
            pltpu.VMEM((ns * T, D), BF16),
            pltpu.VMEM((ns * T, D), F32),
            pltpu.VMEM((2, D, fc), BF16),
            pltpu.VMEM((2, fc, D), BF16),
            pltpu.SemaphoreType.DMA((2, 2)),
        ],
        compiler_params=pltpu.CompilerParams(
            dimension_semantics=("arbitrary",), vmem_limit_bytes=VMEM_LIMIT),
        name="channel_mix",
    )(x, mix, g1, sh2, sc2, g2, wout_b, post_mix_w, pre_ffn_w, post_ffn_w, wup_b, wdn_b)


def _pick(n, pref):
    t = min(n, pref)
    while n % t:
        t //= 2
    return t


def _layer(x, mod, states, lw, ffn_ws, *, pos_base, mix_T, mix_ns, ffn_T, ffn_ns):
    (pre_mix_w, post_mix_w, pre_ffn_w, post_ffn_w, win_b, conv_w, conv_b, dtb, alog, dskip_e,
     normw, inv2) = lw
    B = x.shape[0]
    sh1, sc1, g1, sh2, sc2, g2 = [m.reshape(B, 1, D_MODEL) for m in jnp.split(mod, 6, axis=-1)]
    nf = D_FF // FFN_CHUNK
    cast_ws = tuple((w, nch) for w, nch in zip(ffn_ws, (1, nf, 1)) if w.dtype != BF16)
    res = _mix_call(
        x, sh1, sc1, pre_mix_w, win_b, inv2, conv_w, conv_b, dtb, alog, dskip_e, normw, states,
        cast_ws, T=mix_T, ns=mix_ns, pos_base=pos_base)
    mix, r_new, s_new, c_new = res[:4]
    if cast_ws:
        ffn_ws = tuple(res[4:])
    wout_b, wup_b, wdn_b = ffn_ws
    y = _ffn_call(x, mix, g1, sh2, sc2, g2, wout_b, post_mix_w, pre_ffn_w, post_ffn_w, wup_b, wdn_b,
                  T=ffn_T, ns=ffn_ns, fc=FFN_CHUNK)
    return y, r_new, s_new.reshape(B, SSM_HEADS, SSM_HEADDIM, SSM_STATE), c_new, ffn_ws


def kernel(x_prompt, x_sample, state_ret, state_ssm, state_conv, c_prompt, c_sample, w_ada, b_ada,
           pre_mix_w, post_mix_w, pre_ffn_w, post_ffn_w, w_in, conv_w, conv_b, dt_bias, a_log, d_skip,
           ssm_norm_w, w_out, w_up, w_down):
    depth = w_ada.shape[0]
    bp, lp, _ = x_prompt.shape
    bs, ls, _ = x_sample.shape
    half = RET_DIM // 2
    inv = ROPE_BASE ** (-jnp.arange(half, dtype=F32) / half)
    inv2 = jnp.concatenate([inv, inv]).reshape(1, RET_DIM)

    def pad_lanes(v):
        return jnp.pad(v.astype(F32), (0, LANES - v.shape[0])).reshape(1, LANES)

    yp, ys = x_prompt, x_sample
    outs = [[] for _ in range(6)]
    rows = bp + bs
    rows_pad = -(-rows // 8) * 8
    for l in range(depth):
        c_all = jnp.pad(jnp.concatenate([c_prompt, c_sample], axis=0), ((0, rows_pad - rows), (0, 0)))
        mod = _modulation(c_all, w_ada[l], b_ada[l])
        win_b = w_in[l].astype(BF16)
        lw = (pre_mix_w[l].reshape(1, -1), post_mix_w[l].reshape(1, -1), pre_ffn_w[l].reshape(1, -1),
              post_ffn_w[l].reshape(1, -1), win_b, conv_w[l], conv_b[l].reshape(1, -1),
              pad_lanes(dt_bias[l]), pad_lanes(a_log[l]),
              jnp.repeat(d_skip[l].astype(F32), SSM_HEADDIM).reshape(1, -1),
              ssm_norm_w[l].reshape(1, -1), inv2)
        mix_T = _pick(lp, 256)
        ffn_ws = (w_out[l], w_up[l], w_down[l])
        n_steps = bp * (lp // mix_T)
        if any(w.shape[0] % (16 * n_steps) for w in ffn_ws):
            nf = D_FF // FFN_CHUNK
            ffn_ws = (w_out[l].astype(BF16),
                      jnp.swapaxes(w_up[l].astype(BF16).reshape(D_MODEL, nf, FFN_CHUNK), 0, 1),
                      w_down[l].astype(BF16))
        yp, r, s, c, ffn_ws = _layer(yp, mod[:bp], None, lw, ffn_ws, pos_base=0,
                                     mix_T=mix_T, mix_ns=1, ffn_T=_pick(lp, 512), ffn_ns=1)
        outs[0].append(r); outs[1].append(s); outs[2].append(c)
        ns_mix = _pick(bs, max(1, 128 // ls))
        ns_ffn = _pick(bs, max(1, 512 // ls))
        st = (state_ret[l], state_ssm[l].reshape(bs, SSM_WIDTH, SSM_STATE), state_conv[l])
        ys, r, s, c, _ = _layer(ys, mod[bp:bp + bs], st, lw, ffn_ws, pos_base=PAST_LEN,
                                mix_T=ls, mix_ns=ns_mix, ffn_T=ls, ffn_ns=ns_ffn)
        outs[3].append(r); outs[4].append(s); outs[5].append(c)
    return (yp, ys) + tuple(jnp.stack(o) for o in outs)
```

```python
import functools
import math

import jax
import jax.numpy as jnp
from jax import lax
from jax.experimental import pallas as pl
from jax.experimental.pallas import tpu as pltpu

F32 = jnp.float32
BF16 = jnp.bfloat16

D_MODEL = 2048
PAST_LEN = 4096
CHUNK = 64
RET_HEADS = 8
RET_DIM = 128
RET_WIDTH = RET_HEADS * RET_DIM
SSM_HEADS = 16
SSM_HEADDIM = 64
SSM_WIDTH = SSM_HEADS * SSM_HEADDIM
SSM_GROUPS = 2
HEADS_PER_GROUP = SSM_HEADS // SSM_GROUPS
SSM_STATE = 128
CONV_WIDTH = 4
CONV_DIM = SSM_WIDTH + 2 * SSM_GROUPS * SSM_STATE
D_FF = 4 * D_MODEL
ROPE_BASE = 10000.0
EPS = 1e-6

LANES = 128
CONV_PAD = 8
OFF_Q = 0
OFF_K = RET_WIDTH
OFF_V = 2 * RET_WIDTH
OFF_G = 3 * RET_WIDTH
OFF_Z = 4 * RET_WIDTH
OFF_XBC = 4 * RET_WIDTH + SSM_WIDTH
OFF_DT = OFF_XBC + CONV_DIM
IN_COLS = OFF_DT + SSM_HEADS
PROJ_BLOCK = 512
FFN_CHUNK = 1024
FFN_ROWS = 16

VMEM_LIMIT = 60 * 1024 * 1024


def _silu(x):
    h = 0.5 * x
    return h + h * jnp.tanh(h)


def _softplus(x):
    return jnp.maximum(x, 0.0) + jnp.log1p(jnp.exp(-jnp.abs(x)))


def _split3(x):
    hi = x.astype(BF16)
    r1 = x - hi.astype(F32)
    mid = r1.astype(BF16)
    lo = (r1 - mid.astype(F32)).astype(BF16)
    return hi, mid, lo


def _mod_kernel(c_ref, w_ref, b_ref, o_ref):
    s = _silu(c_ref[...]).astype(BF16)
    o_ref[...] = jnp.dot(s, w_ref[...].astype(BF16), preferred_element_type=F32) + b_ref[...]


def _modulation(c, w_ada, b_ada):
    rows, d = c.shape
    n = w_ada.shape[1]
    bn = 1024
    return pl.pallas_call(
        _mod_kernel,
        grid=(n // bn,),
        in_specs=[
            pl.BlockSpec((rows, d), lambda j: (0, 0)),
            pl.BlockSpec((d, bn), lambda j: (0, j)),
            pl.BlockSpec((1, bn), lambda j: (0, j)),
        ],
        out_specs=pl.BlockSpec((rows, bn), lambda j: (0, j)),
        out_shape=jax.ShapeDtypeStruct((rows, n), F32),
        compiler_params=pltpu.CompilerParams(
            dimension_semantics=("arbitrary",), vmem_limit_bytes=VMEM_LIMIT),
        name="adaln_mod",
    )(c, w_ada, b_ada.reshape(1, n))


def _mix_kernel(*refs, T, ns, cl, pos_base, has_state, n_cast):
    it = iter(refs)
    x_ref, sh_ref, sc_ref, pw_ref, win_ref, inv_ref = (next(it) for _ in range(6))
    cw_ref, cb_ref, dtb_ref, alog_ref, dsk_ref, nw_ref = (next(it) for _ in range(6))
    if has_state:
        sret_ref, sssm_ref, sconv_ref = (next(it) for _ in range(3))
    cast_in = [next(it) for _ in range(n_cast)]
    mix_ref, ret_ref, ssm_ref, conv_ref = (next(it) for _ in range(4))
    cast_out = [next(it) for _ in range(n_cast)]
    (hb_s, q_s, k_s, kd_s, v_s, gz_s, ext_s, act_s, dt_s, ht_s,
     mask_s, qdec_s, kdec_s) = (next(it) for _ in range(13))

    b_idx = pl.program_id(0)
    t_idx = pl.program_id(1)
    R = ns * T
    lg = [math.log(1.0 - 2.0 ** (-5.0 - h)) for h in range(RET_HEADS)]

    for wi, wo in zip(cast_in, cast_out):
        wo[...] = wi[...].astype(BF16)

    @pl.when((b_idx == 0) & (t_idx == 0))
    def _tables():
        ii = lax.broadcasted_iota(jnp.int32, (T, T), 0)
        jj = lax.broadcasted_iota(jnp.int32, (T, T), 1)
        dist = jnp.abs(ii - jj).astype(F32)
        shift = int(math.log2(cl))
        visible = (jj >> shift) <= (ii >> shift)
        ri = lax.broadcasted_iota(jnp.int32, (T, LANES), 0).astype(F32)
        for h in range(RET_HEADS):
            mask_s[h] = jnp.where(visible, jnp.exp(lg[h] * dist), 0.0)
            qdec_s[h] = jnp.exp(lg[h] * (ri + 1.0))
            kdec_s[h] = jnp.exp(lg[h] * (T - 1.0 - ri))

    @pl.when(t_idx == 0)
    def _init():
        if has_state:
            ret_ref[...] = sret_ref[...]
            for s in range(ns):
                for pair in range(SSM_HEADS // 2):
                    pc = slice(pair * LANES, (pair + 1) * LANES)
                    ht_s[s, :, pc] = sssm_ref[s, pc, :].T
        else:
            ret_ref[...] = jnp.zeros(ret_ref.shape, F32)
            ht_s[...] = jnp.zeros(ht_s.shape, F32)
        ext_s[:, 0:CONV_PAD, :] = jnp.zeros((ns, CONV_PAD, CONV_DIM), F32)
        dt_s[...] = jnp.zeros(dt_s.shape, F32)
        if has_state:
            ext_s[:, CONV_PAD - (CONV_WIDTH - 1):CONV_PAD, :] = sconv_ref[...]

    pj = {}

    def proj_prologue():
        scale = pw_ref[...] * (1.0 + sc_ref[...])
        rc = min(T, 64)
        for r0 in range(0, T, rc):
            x3 = x_ref[:, r0:r0 + rc, :]
            ms = jnp.mean(x3 * x3, axis=-1, keepdims=True)
            hmod = x3 * lax.rsqrt(ms + EPS) * scale + sh_ref[...]
            for s in range(ns):
                hb_s[s * T + r0:s * T + r0 + rc, :] = hmod[s].astype(BF16)

    def rotary_tables():
        hr = R // 2
        row = lax.broadcasted_iota(jnp.int32, (hr, LANES), 0)
        low = lax.broadcasted_iota(jnp.int32, (hr, LANES), 1) < RET_DIM // 2
        row = jnp.where(low, row, row + hr)
        pos = (pos_base + t_idx * T + (row & (T - 1))).astype(F32)
        ang = pos * inv_ref[...]
        c2, s2 = jnp.cos(ang), jnp.sin(ang)
        c2r, s2r = pltpu.roll(c2, RET_DIM // 2, 1), pltpu.roll(s2, RET_DIM // 2, 1)
        pj["cos"] = jnp.concatenate([jnp.where(low, c2, c2r), jnp.where(low, c2r, c2)], axis=0)
        pj["sin"] = jnp.concatenate([jnp.where(low, -s2, s2r), jnp.where(low, -s2r, s2)], axis=0)

    def rot(u):
        return u * pj["cos"] + pltpu.roll(u, RET_DIM // 2, 1) * pj["sin"]

    def put(dst, c0, val):
        w = val.shape[1]
        for s in range(ns):
            dst[s, :, c0:c0 + w] = val[s * T:(s + 1) * T]

    def proj_block(c0):
        res = jnp.dot(hb_s[...], win_ref[:, c0:c0 + PROJ_BLOCK], preferred_element_type=F32)
        if c0 < OFF_V:
            for j in range(PROJ_BLOCK // RET_DIM):
                cj = c0 + j * RET_DIM
                r = rot(res[:, j * RET_DIM:(j + 1) * RET_DIM])
                if c0 < OFF_K:
                    put(q_s, cj, r.astype(BF16))
                else:
                    kf = r * (RET_DIM ** -0.5)
                    h = (cj - OFF_K) // RET_DIM
                    put(k_s, cj - OFF_K, kf.astype(BF16))
                    for s in range(ns):
                        kd_s[s, :, cj - OFF_K:cj - OFF_K + RET_DIM] = (
                            kf[s * T:(s + 1) * T] * kdec_s[h]).astype(BF16)
        elif c0 < OFF_G:
            put(v_s, c0 - OFF_V, res.astype(BF16))
        elif c0 < OFF_XBC:
            put(gz_s, c0 - OFF_G, _silu(res))
        else:
            for s in range(ns):
                ext_s[s, CONV_PAD:CONV_PAD + T, c0 - OFF_XBC:c0 - OFF_XBC + PROJ_BLOCK] = (
                    res[s * T:(s + 1) * T])

    def proj_dt():
        res = jnp.dot(hb_s[...], win_ref[:, OFF_DT:IN_COLS], preferred_element_type=F32)
        put(dt_s, 0, _softplus(res + dtb_ref[:, 0:SSM_HEADS]))

    proj_units = [proj_prologue]
    proj_units += [functools.partial(proj_block, c0) for c0 in range(0, OFF_DT, PROJ_BLOCK)]
    proj_units += [proj_dt]

    def mixer_units(s, slot=0):
        mx = {}
        units = []

        def consts():
            ti = lax.broadcasted_iota(jnp.int32, (T, T), 0)
            tj = lax.broadcasted_iota(jnp.int32, (T, T), 1)
            mx["causal"] = tj <= ti
            mx["tri"] = mx["causal"].astype(BF16)
            mx["low"] = lax.broadcasted_iota(jnp.int32, (T, LANES), 1) < SSM_HEADDIM
        units.append(consts)

        def ret_head(h):
            cs = slice(h * RET_DIM, (h + 1) * RET_DIM)
            qh = q_s[s, :, cs]
            kh = k_s[s, :, cs]
            kdh = kd_s[s, :, cs]
            vh = v_s[s, :, cs]
            S_old = ret_ref[s, h]
            sc = lax.dot_general(qh, kh, (((1,), (1,)), ((), ())), preferred_element_type=F32)
            p = (sc * mask_s[h]).astype(BF16)
            o = jnp.dot(p, vh, preferred_element_type=F32)
            o = o + jnp.dot(qh, S_old.astype(BF16), preferred_element_type=F32) * qdec_s[h]
            upd = lax.dot_general(kdh, vh, (((0,), (0,)), ((), ())), preferred_element_type=F32)
            ret_ref[s, h] = math.exp(lg[h] * T) * S_old + upd
            mu = jnp.mean(o, axis=-1, keepdims=True)
            d = o - mu
            var = jnp.mean(d * d, axis=-1, keepdims=True)
            mix_ref[s, :, cs] = (d * lax.rsqrt(var + EPS) * gz_s[s, :, cs]).astype(BF16)
        units += [functools.partial(ret_head, h) for h in range(RET_HEADS)]

        def conv_block(c0):
            cc = slice(c0, c0 + PROJ_BLOCK)
            base = CONV_PAD - (CONV_WIDTH - 1)
            acc = cb_ref[:, cc] + cw_ref[0:1, cc] * ext_s[s, base:base + T, cc]
            for j in range(1, CONV_WIDTH):
                acc = acc + cw_ref[j:j + 1, cc] * ext_s[s, base + j:base + j + T, cc]
            act_s[slot, :, cc] = _silu(acc)
        units += [functools.partial(conv_block, c0) for c0 in range(0, CONV_DIM, PROJ_BLOCK)]

        def conv_tail():
            conv_ref[s] = ext_s[s, CONV_PAD + T - (CONV_WIDTH - 1):CONV_PAD + T, :]
            ext_s[s, 0:CONV_PAD, :] = ext_s[s, T:T + CONV_PAD, :]

        def ssd_cum():
            conv_tail()
            dt = dt_s[s]
            a = dt * (-jnp.exp(alog_ref[...]))
            a_hi, a_mid, a_lo = _split3(a)
            tri = mx["tri"]
            cum = (jnp.dot(tri, a_hi, preferred_element_type=F32)
                   + jnp.dot(tri, a_mid, preferred_element_type=F32)
                   + jnp.dot(tri, a_lo, preferred_element_type=F32))
            mx["cum"] = cum
            mx["cum_t"] = cum.T
            mx["dt_t"] = dt.T
        units.append(ssd_cum)

        gcols = HEADS_PER_GROUP * SSM_HEADDIM

        def ssd_group(g):
            b_f = act_s[slot, :, SSM_WIDTH + g * SSM_STATE:SSM_WIDTH + (g + 1) * SSM_STATE]
            b_g = b_f.astype(BF16)
            c_g = act_s[slot, :, SSM_WIDTH + (SSM_GROUPS + g) * SSM_STATE:
                        SSM_WIDTH + (SSM_GROUPS + g + 1) * SSM_STATE].astype(BF16)
            mx["b_t"] = b_f.T
            mx["gmat"] = lax.dot_general(c_g, b_g, (((1,), (1,)), ((), ())),
                                         preferred_element_type=F32)
            h_old = ht_s[s, :, g * gcols:(g + 1) * gcols]
            mx["y_int"] = jnp.dot(c_g, h_old.astype(BF16), preferred_element_type=F32)

        def ssd_pair(g, m):
            causal, low = mx["causal"], mx["low"]
            cum, cum_t, dt_t = mx["cum"], mx["cum_t"], mx["dt_t"]
            pair = g * (HEADS_PER_GROUP // 2) + m
            h0 = 2 * pair
            pc = slice(pair * LANES, (pair + 1) * LANES)
            ws, bcs, es, decs = [], [], [], []
            for hh in (h0, h0 + 1):
                cbh = jnp.broadcast_to(cum[:, hh:hh + 1], (T, LANES))
                if T < LANES:
                    decay = (jnp.where(causal, jnp.exp(cbh[:, :T] - cum_t[hh:hh + 1, :]), 0.0)
                             * dt_t[hh:hh + 1, :])
                    ws.append((decay * mx["gmat"]).astype(BF16))
                    coef = decay[T - 1:T, :]
                else:
                    nb = T // LANES
                    diag = causal[0:LANES, 0:LANES]
                    wcols, coefs = [], []
                    for c in range(nb):
                        cc = slice(c * LANES, (c + 1) * LANES)
                        wrows = [jnp.zeros((c * LANES, LANES), BF16)] if c else []
                        for r in range(c, nb):
                            rr = slice(r * LANES, (r + 1) * LANES)
                            e = jnp.exp(cbh[rr] - cum_t[hh:hh + 1, cc])
                            d = (jnp.where(diag, e, 0.0) if r == c else e) * dt_t[hh:hh + 1, cc]
                            wrows.append((d * mx["gmat"][rr, cc]).astype(BF16))
                        wcols.append(jnp.concatenate(wrows, axis=0))
                        coefs.append(d[LANES - 1:LANES, :])
                    ws.append(jnp.concatenate(wcols, axis=1))
                    coef = jnp.concatenate(coefs, axis=1)
                bcs.append((mx["b_t"] * coef).astype(BF16))
                es.append(jnp.exp(cbh))
                decs.append(jnp.exp(cum_t[hh:hh + 1, T - 1:T]))
            xs_p = act_s[slot, :, pc]
            xs_b = xs_p.astype(BF16)
            zero = jnp.zeros_like(xs_b)
            rhs = jnp.concatenate([jnp.where(low, xs_b, zero),
                                   jnp.where(low, zero, xs_b)], axis=0)
            y = jnp.dot(jnp.concatenate(ws, axis=1), rhs, preferred_element_type=F32)
            y = (y + mx["y_int"][:, m * LANES:(m + 1) * LANES] * jnp.where(low, es[0], es[1])
                 + dsk_ref[:, pc] * xs_p)
            act_s[slot, :, pc] = y
            upd = jnp.dot(jnp.concatenate(bcs, axis=1), rhs, preferred_element_type=F32)
            low_n = lax.broadcasted_iota(jnp.int32, (SSM_STATE, LANES), 1) < SSM_HEADDIM
            ht_s[s, :, pc] = jnp.where(low_n, decs[0], decs[1]) * ht_s[s, :, pc] + upd

        def ssd_state(g):
            for m in range(HEADS_PER_GROUP // 2):
                pair = g * (HEADS_PER_GROUP // 2) + m
                pc = slice(pair * LANES, (pair + 1) * LANES)
                ssm_ref[s, pc, :] = ht_s[s, :, pc].T

        for g in range(SSM_GROUPS):
            units.append(functools.partial(ssd_group, g))
            units += [functools.partial(ssd_pair, g, m) for m in range(HEADS_PER_GROUP // 2)]
            units.append(functools.partial(ssd_state, g))

        def ssd_norm(g):
            gw = SSM_WIDTH // SSM_GROUPS
            gc = slice(g * gw, (g + 1) * gw)
            yz = act_s[slot, :, gc] * gz_s[s, :, RET_WIDTH + g * gw:RET_WIDTH + (g + 1) * gw]
            msq = jnp.mean(yz * yz, axis=-1, keepdims=True)
            mix_ref[s, :, RET_WIDTH + g * gw:RET_WIDTH + (g + 1) * gw] = (
                yz * lax.rsqrt(msq + EPS) * nw_ref[:, gc]).astype(BF16)
        units += [functools.partial(ssd_norm, g) for g in range(SSM_GROUPS)]
        return units

    if ns == 1:
        blk = {c0: u for c0, u in zip(range(0, OFF_DT, PROJ_BLOCK), proj_units[1:-1])}
        mu = mixer_units(0)
        proj_units[0]()
        mu[0]()
        for c0 in range(OFF_XBC, OFF_DT, PROJ_BLOCK):
            blk[c0]()
        proj_units[-1]()
        rotary_tables()
        order = list(range(OFF_Q, OFF_XBC, PROJ_BLOCK))
        after = [[9], [10], [11], [12, 13], [14, 15], [16, 17, 18], [19, 20, 21], [22, 23, 24],
                 [1, 2, 3, 4], [25, 26, 5, 6, 7, 8]]
        assert len(order) == len(after) and sorted(sum(after, [0])) == list(range(len(mu)))
        for c0, bis in zip(order, after):
            blk[c0]()
            for bi in bis:
                mu[bi]()
    else:
        rotary_tables()
        for au in proj_units:
            au()

        unroll = act_s.shape[0]

        def seq_body(i, carry):
            streams = [mixer_units(i * unroll + u, u) for u in range(unroll)]
            for step_units in zip(*streams):
                for bu in step_units:
                    bu()
            return carry

        lax.fori_loop(0, ns // unroll, seq_body, 0)


def _mix_call(x, sh1, sc1, pre_w, win_b, inv2, conv_w, conv_b, dtb, alog, dskip_e, normw,
              states, cast_ws, *, T, ns, pos_base):
    B, L, D = x.shape
    has_state = states is not None
    cl = min(L, CHUNK)
    n_t = L // T
    grid = (B // ns, n_t)
    n_steps = grid[0] * grid[1]
    R = ns * T
    unroll = math.gcd(ns, 4)

    def full(shape):
        nd = len(shape)
        return pl.BlockSpec(shape, lambda b, t, _nd=nd: (0,) * _nd, pipeline_mode=pl.Buffered(1))

    in_specs = [
        pl.BlockSpec((ns, T, D), lambda b, t: (b, t, 0)),
        pl.BlockSpec((ns, 1, D), lambda b, t: (b, 0, 0)),
        pl.BlockSpec((ns, 1, D), lambda b, t: (b, 0, 0)),
        full((1, D)),
        full(win_b.shape),
        full((1, LANES)),
        full(conv_w.shape),
        full((1, CONV_DIM)),
        full((1, LANES)),
        full((1, LANES)),
        full((1, SSM_WIDTH)),
        full((1, SSM_WIDTH)),
    ]
    args = [x, sh1, sc1, pre_w, win_b, inv2, conv_w, conv_b, dtb, alog, dskip_e, normw]
    if has_state:
        in_specs += [
            pl.BlockSpec((ns, RET_HEADS, RET_DIM, RET_DIM), lambda b, t: (b, 0, 0, 0)),
            pl.BlockSpec((ns, SSM_WIDTH, SSM_STATE), lambda b, t: (b, 0, 0)),
            pl.BlockSpec((ns, CONV_WIDTH - 1, CONV_DIM), lambda b, t: (b, 0, 0)),
        ]
        args += list(states)
    cast_specs = []
    for w in cast_ws:
        rows = w.shape[0] // n_steps
        assert rows * n_steps == w.shape[0] and rows % 16 == 0
        cast_specs.append(pl.BlockSpec((rows, w.shape[1]), lambda b, t: (b * n_t + t, 0)))
    in_specs += cast_specs
    args += list(cast_ws)
    out_specs = [
        pl.BlockSpec((ns, T, D), lambda b, t: (b, t, 0)),
        pl.BlockSpec((ns, RET_HEADS, RET_DIM, RET_DIM), lambda b, t: (b, 0, 0, 0)),
        pl.BlockSpec((ns, SSM_WIDTH, SSM_STATE), lambda b, t: (b, 0, 0)),
        pl.BlockSpec((ns, CONV_WIDTH - 1, CONV_DIM), lambda b, t: (b, 0, 0)),
    ]
    out_shape = [
        jax.ShapeDtypeStruct((B, L, D), BF16),
        jax.ShapeDtypeStruct((B, RET_HEADS, RET_DIM, RET_DIM), F32),
        jax.ShapeDtypeStruct((B, SSM_WIDTH, SSM_STATE), F32),
        jax.ShapeDtypeStruct((B, CONV_WIDTH - 1, CONV_DIM), F32),
    ]
    out_specs += cast_specs
    out_shape += [jax.ShapeDtypeStruct(w.shape, BF16) for w in cast_ws]
    scratch = [
        pltpu.VMEM((R, D), BF16),
        pltpu.VMEM((ns, T, RET_WIDTH), BF16),
        pltpu.VMEM((ns, T, RET_WIDTH), BF16),
        pltpu.VMEM((ns, T, RET_WIDTH), BF16),
        pltpu.VMEM((ns, T, RET_WIDTH), BF16),
        pltpu.VMEM((ns, T, 2 * RET_WIDTH), F32),
        pltpu.VMEM((ns, T + CONV_PAD, CONV_DIM), F32),
        pltpu.VMEM((unroll, T, CONV_DIM), F32),
        pltpu.VMEM((ns, T, LANES), F32),
        pltpu.VMEM((ns, SSM_STATE, SSM_WIDTH), F32),
        pltpu.VMEM((RET_HEADS, T, T), F32),
        pltpu.VMEM((RET_HEADS, T, LANES), F32),
        pltpu.VMEM((RET_HEADS, T, LANES), F32),
    ]
    kern = functools.partial(_mix_kernel, T=T, ns=ns, cl=cl, pos_base=pos_base, has_state=has_state,
                             n_cast=len(cast_ws))
    return pl.pallas_call(
        kern,
        grid=grid,
        in_specs=in_specs,
        out_specs=out_specs,
        out_shape=out_shape,
        scratch_shapes=scratch,
        compiler_params=pltpu.CompilerParams(
            dimension_semantics=("arbitrary", "arbitrary"), vmem_limit_bytes=VMEM_LIMIT),
        name="token_mix_state" if has_state else "token_mix",
    )(*args)


def _ffn_kernel(x_ref, mix_ref, g1_ref, sh_ref, sc_ref, g2_ref, wout_ref, pmw_ref, pfw_ref, qfw_ref,
                wup_hbm, wdn_hbm, o_ref, h2_s, acc_s, wup_buf, wdn_buf, sem, *, ns, T, nf, fc):
    i = pl.program_id(0)
    n_tiles = pl.num_programs(0)
    R = ns * T

    def chunk_copies(f, slot):
        return (pltpu.make_async_copy(wup_hbm.at[:, pl.ds(f * fc, fc)], wup_buf.at[slot], sem.at[0, slot]),
                pltpu.make_async_copy(wdn_hbm.at[pl.ds(f * fc, fc), :], wdn_buf.at[slot], sem.at[1, slot]))

    def start(f, slot):
        for prio, c in enumerate(chunk_copies(f, slot)):
            c.start(priority=prio)

    def wait(f, slot):
        for c in chunk_copies(f, slot):
            c.wait()
    rc = min(T, FFN_ROWS)
    halves = ((0, R // 2), (R // 2, R))

    def inv_rms(v):
        return lax.rsqrt(jnp.mean(v * v, axis=-1, keepdims=True) + EPS)

    def passes(lo, hi):
        return [(s, r0) for s in range(ns) for r0 in range(0, T, rc) if lo <= s * T + r0 < hi]

    def out_proj(lo, hi):
        lhs = mix_ref[0, lo:hi, :] if ns == 1 else mix_ref[lo // T:hi // T].reshape(hi - lo, D_MODEL)
        acc_s[lo:hi, :] = jnp.dot(lhs, wout_ref[...], preferred_element_type=F32)

    def first_norms(lo, hi):
        for s, r0 in passes(lo, hi):
            rows = slice(s * T + r0, s * T + r0 + rc)
            m = acc_s[rows, :]
            x1 = x_ref[s, r0:r0 + rc, :] + m * inv_rms(m) * (pmw_ref[...] * g1_ref[s])
            o_ref[s, r0:r0 + rc, :] = x1
            h2_s[rows, :] = (x1 * inv_rms(x1) * (pfw_ref[...] * (1.0 + sc_ref[s])) + sh_ref[s]).astype(BF16)
            acc_s[rows, :] = jnp.zeros((rc, D_MODEL), F32)

    def mlp(lo, hi, slot):
        u = jnp.dot(h2_s[lo:hi, :], wup_buf[slot], preferred_element_type=F32)
        u = jnp.maximum(u, 0.0)
        u = (u * u).astype(BF16)
        acc_s[lo:hi, :] += jnp.dot(u, wdn_buf[slot], preferred_element_type=F32)

    def last_norms(lo, hi):
        for s, r0 in passes(lo, hi):
            a = acc_s[s * T + r0:s * T + r0 + rc, :]
            o_ref[s, r0:r0 + rc, :] = (o_ref[s, r0:r0 + rc, :]
                                       + a * inv_rms(a) * (qfw_ref[...] * g2_ref[s]))

    @pl.when(i == 0)
    def _prime():
        start(0, 0)

    wait(0, 0)
    start(1, 1)
    for lo, hi in halves:
        out_proj(lo, hi)
    for lo, hi in halves:
        first_norms(lo, hi)
        mlp(lo, hi, 0)

    def middle(f, carry):
        slot = f % 2
        wait(f, slot)
        start(f + 1, 1 - slot)
        mlp(0, R, slot)
        return carry

    lax.fori_loop(1, nf - 1, middle, 0)

    last_slot = (nf - 1) % 2
    wait(nf - 1, last_slot)

    @pl.when(i + 1 < n_tiles)
    def _next_tile():
        start(0, 0)

    for lo, hi in halves:
        mlp(lo, hi, last_slot)
    for lo, hi in halves:
        last_norms(lo, hi)


def _ffn_call(x, mix, g1, sh2, sc2, g2, wout_b, post_mix_w, pre_ffn_w, post_ffn_w, wup_b, wdn_b,
              *, T, ns, fc):
    B, L, D = x.shape
    nt = L // T
    grid = (B // ns * nt,)
    nf = D_FF // fc

    def tok(i):
        return (i // nt, i % nt, 0)

    def seq(i):
        return (i // nt, 0, 0)

    def full(shape):
        return pl.BlockSpec(shape, lambda i: (0, 0), pipeline_mode=pl.Buffered(1))

    assert nf >= 2 and nf % 2 == 0 and (ns * T) % 32 == 0
    kern = functools.partial(_ffn_kernel, ns=ns, T=T, nf=nf, fc=fc)
    return pl.pallas_call(
        kern,
        grid=grid,
        in_specs=[
            pl.BlockSpec((ns, T, D), tok),
            pl.BlockSpec((ns, T, D), tok),
            pl.BlockSpec((ns, 1, D), seq),
            pl.BlockSpec((ns, 1, D), seq),
            pl.BlockSpec((ns, 1, D), seq),
            pl.BlockSpec((ns, 1, D), seq),
            full((D, D)),
            full((1, D)),
            full((1, D)),
            full((1, D)),
            pl.BlockSpec(memory_space=pl.ANY),
            pl.BlockSpec(memory_space=pl.ANY),
        ],
        out_specs=pl.BlockSpec((ns, T, D), tok),
        out_shape=jax.ShapeDtypeStruct((B, L, D), F32),
        scratch_shapes=[
            pltpu.VMEM((ns * T, D), BF16),
            pltpu.VMEM((ns * T, D), F32),
            pltpu.VMEM((2, D, fc), BF16),
            pltpu.VMEM((2, fc, D), BF16),
            pltpu.SemaphoreType.DMA((2, 2)),
        ],
        compiler_params=pltpu.CompilerParams(
            dimension_semantics=("arbitrary",), vmem_limit_bytes=VMEM_LIMIT),
        name="channel_mix",
    )(x, mix, g1, sh2, sc2, g2, wout_b, post_mix_w, pre_ffn_w, post_ffn_w, wup_b, wdn_b)


def _pick(n, pref):
    t = min(n, pref)
    while n % t:
        t //= 2
    return t


def _layer(x, mod, states, lw, ffn_ws, *, pos_base, mix_T, mix_ns, ffn_T, ffn_ns):
    (pre_mix_w, post_mix_w, pre_ffn_w, post_ffn_w, win_b, conv_w, conv_b, dtb, alog, dskip_e,
     normw, inv2) = lw
    B = x.shape[0]
    sh1, sc1, g1, sh2, sc2, g2 = [m.reshape(B, 1, D_MODEL) for m in jnp.split(mod, 6, axis=-1)]
    cast_ws = tuple(w for w in ffn_ws if w.dtype != BF16)
    res = _mix_call(
        x, sh1, sc1, pre_mix_w, win_b, inv2, conv_w, conv_b, dtb, alog, dskip_e, normw, states,
        cast_ws, T=mix_T, ns=mix_ns, pos_base=pos_base)
    mix, r_new, s_new, c_new = res[:4]
    if cast_ws:
        ffn_ws = tuple(res[4:])
    wout_b, wup_b, wdn_b = ffn_ws
    y = _ffn_call(x, mix, g1, sh2, sc2, g2, wout_b, post_mix_w, pre_ffn_w, post_ffn_w, wup_b, wdn_b,
                  T=ffn_T, ns=ffn_ns, fc=FFN_CHUNK)
    return y, r_new, s_new.reshape(B, SSM_HEADS, SSM_HEADDIM, SSM_STATE), c_new, ffn_ws


def kernel(x_prompt, x_sample, state_ret, state_ssm, state_conv, c_prompt, c_sample, w_ada, b_ada,
           pre_mix_w, post_mix_w, pre_ffn_w, post_ffn_w, w_in, conv_w, conv_b, dt_bias, a_log, d_skip,
           ssm_norm_w, w_out, w_up, w_down):
    depth = w_ada.shape[0]
    bp, lp, _ = x_prompt.shape
    bs, ls, _ = x_sample.shape
    half = RET_DIM // 2
    inv = ROPE_BASE ** (-jnp.arange(half, dtype=F32) / half)
    inv2 = jnp.concatenate([inv, inv]).reshape(1, RET_DIM)

    def pad_lanes(v):
        return jnp.pad(v.astype(F32), (0, LANES - v.shape[0])).reshape(1, LANES)

    yp, ys = x_prompt, x_sample
    outs = [[] for _ in range(6)]
    rows = bp + bs
    rows_pad = -(-rows // 8) * 8
    for l in range(depth):
        c_all = jnp.pad(jnp.concatenate([c_prompt, c_sample], axis=0), ((0, rows_pad - rows), (0, 0)))
        mod = _modulation(c_all, w_ada[l], b_ada[l])
        win_b = w_in[l].astype(BF16)
        lw = (pre_mix_w[l].reshape(1, -1), post_mix_w[l].reshape(1, -1), pre_ffn_w[l].reshape(1, -1),
              post_ffn_w[l].reshape(1, -1), win_b, conv_w[l], conv_b[l].reshape(1, -1),
              pad_lanes(dt_bias[l]), pad_lanes(a_log[l]),
              jnp.repeat(d_skip[l].astype(F32), SSM_HEADDIM).reshape(1, -1),
              ssm_norm_w[l].reshape(1, -1), inv2)
        mix_T = _pick(lp, 256)
        ffn_ws = (w_out[l], w_up[l], w_down[l])
        n_steps = bp * (lp // mix_T)
        if any(w.shape[0] % (16 * n_steps) for w in ffn_ws):
            ffn_ws = tuple(w.astype(BF16) for w in ffn_ws)
        yp, r, s, c, ffn_ws = _layer(yp, mod[:bp], None, lw, ffn_ws, pos_base=0,
                                     mix_T=mix_T, mix_ns=1, ffn_T=_pick(lp, 512), ffn_ns=1)
        outs[0].append(r); outs[1].append(s); outs[2].append(c)
        ns_mix = _pick(bs, max(1, 128 // ls))
        ns_ffn = _pick(bs, max(1, 512 // ls))
        st = (state_ret[l], state_ssm[l].reshape(bs, SSM_WIDTH, SSM_STATE), state_conv[l])
        ys, r, s, c, _ = _layer(ys, mod[bp:bp + bs], st, lw, ffn_ws, pos_base=PAST_LEN,
                                mix_T=ls, mix_ns=ns_mix, ffn_T=ls, ffn_ns=ns_ffn)
        outs[3].append(r); outs[4].append(s); outs[5].append(c)
    return (yp, ys) + tuple(jnp.stack(o) for o in outs)
```
